```python
import math
import jax
import jax.numpy as jnp
from jax import lax
import numpy as np

D_MODEL = 1024
BATCH = 2
SEQ = 8192
DEPTH = 2

GRID_W = 64
CTX_LEN = 256
HEAD_DIM = 64
A_HEADS = 8
A_KV_HEADS = 2
B_HEADS = 8
WIN_R = 8
WIN_C = 16
C_HEADS = 4
C_DK = 128
C_DV = 128
CONV_K = 5
CHUNK = 64
Q_BLOCK = 128
D_FF = 4 * D_MODEL
N_BRANCH = 3
ROPE_BASE = 10000.0
EPS = 1e-6

A_Q = A_HEADS * HEAD_DIM
A_KV = A_KV_HEADS * HEAD_DIM
B_W = B_HEADS * HEAD_DIM
C_K = C_HEADS * C_DK
C_V = C_HEADS * C_DV
BRANCH_W = A_Q
IN_SPLITS = (A_Q, A_KV, A_KV, B_W, B_W, B_W, 2 * C_K + C_V, C_V, 2 * C_HEADS, 2 * C_HEADS, N_BRANCH * D_MODEL)
IN_OFFSETS = [int(o) for o in np.cumsum(IN_SPLITS)[:-1]]
N_IN = sum(IN_SPLITS)
F32 = jnp.float32

kernel_name = 'hybrid_dit_gqa_natten_gdelta'


def rms_norm(x, gain):
    xf = x.astype(F32)
    y = xf * lax.rsqrt(jnp.mean(xf * xf, axis=-1, keepdims=True) + EPS)
    return (y * gain.astype(F32)).astype(x.dtype)


def l2_norm(x):
    xf = x.astype(F32)
    return (xf * lax.rsqrt(jnp.sum(xf * xf, axis=-1, keepdims=True) + EPS)).astype(x.dtype)


def modulate(h, shift, scale):
    return h * (1 + scale) + shift


def heads(a, n):
    return a.reshape(*a.shape[:-1], n, a.shape[-1] // n)


def axial_rope(n_tok):
    t = jnp.arange(n_tok)
    row = (t // GRID_W).astype(F32)
    col = (t % GRID_W).astype(F32)
    axis_dim = HEAD_DIM // 2
    freqs = ROPE_BASE ** (-jnp.arange(0, axis_dim, 2, dtype=F32) / axis_dim)
    ang = jnp.concatenate([row[:, None] * freqs, col[:, None] * freqs], axis=-1)
    return jnp.cos(ang), jnp.sin(ang)


def apply_rope(x, cos, sin):
    xf = x.astype(F32).reshape(*x.shape[:-1], HEAD_DIM // 2, 2)
    x0, x1 = xf[..., 0], xf[..., 1]
    c, s = cos[:, None, :], sin[:, None, :]
    out = jnp.stack([x0 * c - x1 * s, x0 * s + x1 * c], axis=-1)
    return out.reshape(x.shape).astype(x.dtype)


def gqa_attend(q, k, v):
    s = jnp.einsum('bqhgd,bkhd->bhgqk', q, k).astype(F32) * (q.shape[-1] ** -0.5)
    p = jax.nn.softmax(s, axis=-1).astype(v.dtype)
    o = jnp.einsum('bhgqk,bkhd->bqhgd', p, v)
    return o.reshape(*o.shape[:2], -1)


def gqa_latent(q, k, v, kc, vc):
    B, S, H, dh = q.shape
    hkv = k.shape[2]
    k_all = jnp.concatenate([k, kc], axis=1)
    v_all = jnp.concatenate([v, vc], axis=1)
    qb = q.reshape(B, S // Q_BLOCK, Q_BLOCK, hkv, H // hkv, dh).swapaxes(0, 1)
    o = lax.map(lambda qi: gqa_attend(qi, k_all, v_all), qb)
    return o.swapaxes(0, 1).reshape(B, S, H * dh)


def neighbourhood_index(n_tok):
    rows = n_tok // GRID_W
    win_r = min(WIN_R, rows)
    t = jnp.arange(n_tok)
    r, col = t // GRID_W, t % GRID_W
    r0 = jnp.clip(r - win_r // 2, 0, rows - win_r)
    c0 = jnp.clip(col - WIN_C // 2, 0, GRID_W - WIN_C)
    kr = r0[:, None, None] + jnp.arange(win_r)[None, :, None]
    kcl = c0[:, None, None] + jnp.arange(WIN_C)[None, None, :]
    idx = (kr * GRID_W + kcl).reshape(n_tok, win_r * WIN_C)
    bidx = ((kr - r[:, None, None] + WIN_R - 1) * (2 * WIN_C - 1)
            + (kcl - col[:, None, None] + WIN_C - 1)).reshape(n_tok, win_r * WIN_C)
    return idx, bidx


def neighbourhood_latent(q, k, v, kc, vc, rpb):
    B, S, H, dh = q.shape
    idx, bidx = neighbourhood_index(S)
    nb = S // Q_BLOCK
    rpb_flat = rpb.reshape(H, -1)
    scale = dh ** -0.5

    def block(args):
        qi, ii, bi = args
        kg = jnp.take(k, ii, axis=1)
        vg = jnp.take(v, ii, axis=1)
        s_loc = jnp.einsum('bqhd,bqkhd->bhqk', qi, kg).astype(F32) * scale + rpb_flat[:, bi].astype(F32)[None]
        s_ctx = jnp.einsum('bqhd,bchd->bhqc', qi, kc).astype(F32) * scale
        p = jax.nn.softmax(jnp.concatenate([s_loc, s_ctx], axis=-1), axis=-1).astype(v.dtype)
        nk = ii.shape[-1]
        return (jnp.einsum('bhqk,bqkhd->bqhd', p[..., :nk], vg)
                + jnp.einsum('bhqc,bchd->bqhd', p[..., nk:], vc))

    qb = q.reshape(B, nb, Q_BLOCK, H, dh).swapaxes(0, 1)
    o = lax.map(block, (qb, idx.reshape(nb, Q_BLOCK, -1), bidx.reshape(nb, Q_BLOCK, -1)))
    return o.swapaxes(0, 1).reshape(B, S, H * dh)


def short_conv(x, w):
    y = lax.conv_general_dilated(x, w[:, None, :].astype(x.dtype), window_strides=(1,),
                                 padding=[(CONV_K // 2, CONV_K // 2)],
                                 dimension_numbers=('NWC', 'WIO', 'NWC'),
                                 feature_group_count=x.shape[-1])
    return jax.nn.silu(y)


def gdn_features(qkv, beta_raw, a_raw, conv_w, a_log, dt_bias):
    qkv = short_conv(qkv, conv_w)
    q, k, v = jnp.split(qkv, [C_K, 2 * C_K], axis=-1)
    q = l2_norm(heads(q, C_HEADS))
    k = l2_norm(heads(k, C_HEADS))
    v = heads(v, C_HEADS)
    beta = jax.nn.sigmoid(heads(beta_raw, 2).astype(F32))
    g = -jnp.exp(a_log.astype(F32)) * jax.nn.softplus(heads(a_raw, 2).astype(F32) + dt_bias.astype(F32))
    return q, k, v, beta, g


def gated_delta_chunked(q, k, v, beta, g, state0, with_output):
    B, T, H, dk = k.shape
    dv = v.shape[-1]
    n = T // CHUNK

    def chunks(a):
        a = a.astype(F32).reshape(B, n, CHUNK, H, *a.shape[3:])
        return jnp.moveaxis(a, 3, 2).swapaxes(0, 1)

    kc, vc, bc = chunks(k), chunks(v), chunks(beta)
    gc = jnp.cumsum(chunks(g), axis=-1)
    lower = jnp.tril(jnp.ones((CHUNK, CHUNK), dtype=bool))
    strict = jnp.tril(jnp.ones((CHUNK, CHUNK), dtype=bool), -1)
    decay = jnp.exp(jnp.where(lower, gc[..., :, None] - gc[..., None, :], -jnp.inf))
    kb = kc * bc[..., None]
    nil = jnp.where(strict, -jnp.einsum('nbhid,nbhjd->nbhij', kb, kc) * decay, 0.0)
    tinv = jnp.eye(CHUNK, dtype=F32) + nil
    power = nil
    for _ in range(int(math.log2(CHUNK)) - 1):
        power = power @ power
        tinv = tinv + tinv @ power
    w_v = tinv @ (vc * bc[..., None])
    k_cd = tinv @ (kb * jnp.exp(gc)[..., None])
    g_last = gc[..., -1]
    k_dec = kc * jnp.exp(g_last[..., None] - gc)[..., None]
    xs = (w_v, k_cd, k_dec, g_last)
    if with_output:
        qc = chunks(q) * (dk ** -0.5)
        xs = xs + (qc * jnp.exp(gc)[..., None], jnp.einsum('nbhid,nbhjd->nbhij', qc, kc) * decay)

    def step(state, inp):
        wv, kcd, kd, gl = inp[:4]
        v_new = wv - jnp.einsum('bhck,bhkv->bhcv', kcd, state)
        new_state = state * jnp.exp(gl)[..., None, None] + jnp.einsum('bhck,bhcv->bhkv', kd, v_new)
        if with_output:
            qd, aqk = inp[4:]
            o = jnp.einsum('bhck,bhkv->bhcv', qd, state) + jnp.einsum('bhij,bhjv->bhiv', aqk, v_new)
            return new_state, o
        return new_state, None

    state, outs = lax.scan(step, state0, xs)
    if not with_output:
        return None, state
    return outs.transpose(1, 0, 3, 2, 4).reshape(B, T, H, dv), state


def bidir_gated_delta(feat, cfeat, need_ctx):
    q, k, v, beta, g = feat
    cq, ck, cv, cbeta, cg = cfeat
    zero = jnp.zeros((k.shape[0], C_HEADS, C_DK, C_DV), F32)

    def flip(a):
        return jnp.flip(a, axis=1)

    oc_f, s_f = gated_delta_chunked(cq, ck, cv, cbeta[:, :, 0], cg[:, :, 0], zero, need_ctx)
    o_f, _ = gated_delta_chunked(q, k, v, beta[:, :, 0], g[:, :, 0], s_f, True)
    oc_r, s_r = gated_delta_chunked(flip(cq), flip(ck), flip(cv), flip(cbeta[:, :, 1]), flip(cg[:, :, 1]), zero, need_ctx)
    o_r, _ = gated_delta_chunked(flip(q), flip(k), flip(v), flip(beta[:, :, 1]), flip(g[:, :, 1]), s_r, True)
    o = o_f + flip(o_r)
    oc = oc_f + flip(oc_r) if need_ctx else None
    return o, oc


def gdn_output(o, z, o_norm):
    y = rms_norm(o, o_norm) * jax.nn.silu(heads(z, C_HEADS).astype(F32))
    return y.reshape(z.shape).astype(z.dtype)


def merge_branches(o_a, o_b, o_c, gate_raw, w_branch, w_out):
    br = jnp.einsum('btnc,ncd->btnd', jnp.stack([o_a, o_b, o_c], axis=-2), w_branch)
    gates = jax.nn.sigmoid(heads(gate_raw, N_BRANCH))
    return jnp.sum(gates * br, axis=-2) @ w_out


def sq_relu_mlp(h, w1, w2):
    return jnp.square(jax.nn.relu(h @ w1)) @ w2


def trunk_layer(x, xc, mod, mod_c, params, cos, sin, need_ctx):
    (norm_mix, w_in, q_norm_a, k_norm_a, rpb_b, conv_c, a_log_c, dt_bias_c, o_norm_c,
     w_branch, w_out, norm_ffn, w_ffn1, w_ffn2) = params
    sh_m, sc_m, gt_m, sh_f, sc_f, gt_f = jnp.split(mod, 6, axis=-1)
    csh_m, csc_m, cgt_m, csh_f, csc_f, cgt_f = jnp.split(mod_c, 6, axis=-1)

    h = modulate(rms_norm(x, norm_mix), sh_m, sc_m)
    hc = modulate(rms_norm(xc, norm_mix), csh_m, csc_m)
    (qa, ka, va, qb, kb, vb, qkv_c, z_c, beta_c, a_c, gate) = jnp.split(h @ w_in, IN_OFFSETS, axis=-1)
    (cqa, cka, cva, cqb, ckb, cvb, cqkv_c, cz_c, cbeta_c, ca_c, cgate) = jnp.split(hc @ w_in, IN_OFFSETS, axis=-1)

    qa = apply_rope(rms_norm(heads(qa, A_HEADS), q_norm_a), cos, sin)
    ka = apply_rope(rms_norm(heads(ka, A_KV_HEADS), k_norm_a), cos, sin)
    cka = rms_norm(heads(cka, A_KV_HEADS), k_norm_a)
    cva = heads(cva, A_KV_HEADS)
    o_a = gqa_latent(qa, ka, heads(va, A_KV_HEADS), cka, cva)

    ckb = heads(ckb, B_HEADS)
    cvb = heads(cvb, B_HEADS)
    o_b = neighbourhood_latent(heads(qb, B_HEADS), heads(kb, B_HEADS), heads(vb, B_HEADS), ckb, cvb, rpb_b)

    feat = gdn_features(qkv_c, beta_c, a_c, conv_c, a_log_c, dt_bias_c)
    cfeat = gdn_features(cqkv_c, cbeta_c, ca_c, conv_c, a_log_c, dt_bias_c)
    o_c, oc_c = bidir_gated_delta(feat, cfeat, need_ctx)
    o_c = gdn_output(o_c, z_c, o_norm_c)

    x = x + gt_m * merge_branches(o_a, o_b, o_c, gate, w_branch, w_out)
    x = x + gt_f * sq_relu_mlp(modulate(rms_norm(x, norm_ffn), sh_f, sc_f), w_ffn1, w_ffn2)

    if need_ctx:
        B, Lc = xc.shape[:2]
        cqa_h = rms_norm(heads(cqa, A_HEADS), q_norm_a).reshape(B, Lc, A_KV_HEADS, A_HEADS // A_KV_HEADS, HEAD_DIM)
        oc_a = gqa_attend(cqa_h, cka, cva)
        oc_b = gqa_attend(heads(cqb, B_HEADS)[:, :, :, None, :], ckb, cvb)
        oc_c = gdn_output(oc_c, cz_c, o_norm_c)
        xc = xc + cgt_m * merge_branches(oc_a, oc_b, oc_c, cgate, w_branch, w_out)
        xc = xc + cgt_f * sq_relu_mlp(modulate(rms_norm(xc, norm_ffn), csh_f, csc_f), w_ffn1, w_ffn2)
    return x, xc


def setup_inputs(seed: int = 0) -> dict:
    key = jax.random.key(seed)
    ks = jax.random.split(key, 24)

    def nrm(k, shape, scale):
        return scale * jax.random.normal(k, shape, F32)

    def gain(k, shape):
        return 1.0 + 0.02 * jax.random.normal(k, shape, F32)

    dt = jnp.exp(jax.random.uniform(ks[10], (DEPTH, 2, C_HEADS), F32, math.log(1e-3), math.log(1e-1)))
    return {
        'x': nrm(ks[0], (BATCH, SEQ, D_MODEL), 1.0),
        'c': nrm(ks[1], (BATCH, D_MODEL), 1.0),
        'ctx': nrm(ks[2], (BATCH, CTX_LEN, D_MODEL), 1.0),
        'c_ctx': nrm(ks[3], (D_MODEL,), 1.0),
        'w_mod': nrm(ks[4], (DEPTH, D_MODEL, 6 * D_MODEL), 0.5 * D_MODEL ** -0.5),
        'b_mod': nrm(ks[5], (DEPTH, 6 * D_MODEL), 0.01),
        'norm_mix': gain(ks[6], (DEPTH, D_MODEL)),
        'w_in': nrm(ks[7], (DEPTH, D_MODEL, N_IN), D_MODEL ** -0.5),
        'q_norm_a': gain(ks[8], (DEPTH, HEAD_DIM)),
        'k_norm_a': gain(ks[9], (DEPTH, HEAD_DIM)),
        'rpb_b': nrm(ks[11], (DEPTH, B_HEADS, 2 * WIN_R - 1, 2 * WIN_C - 1), 0.1),
        'conv_c': nrm(ks[12], (DEPTH, CONV_K, 2 * C_K + C_V), CONV_K ** -0.5),
        'a_log_c': jnp.log(jax.random.uniform(ks[13], (DEPTH, 2, C_HEADS), F32, 1.0, 16.0)),
        'dt_bias_c': dt + jnp.log(-jnp.expm1(-dt)),
        'o_norm_c': gain(ks[14], (DEPTH, C_DV)),
        'w_branch': nrm(ks[15], (DEPTH, N_BRANCH, BRANCH_W, D_MODEL), BRANCH_W ** -0.5),
        'w_out': nrm(ks[16], (DEPTH, D_MODEL, D_MODEL), D_MODEL ** -0.5),
        'norm_ffn': gain(ks[17], (DEPTH, D_MODEL)),
        'w_ffn1': nrm(ks[18], (DEPTH, D_MODEL, D_FF), D_MODEL ** -0.5),
        'w_ffn2': nrm(ks[19], (DEPTH, D_FF, D_MODEL), D_FF ** -0.5),
        'norm_final': gain(ks[20], (D_MODEL,)),
    }


def reference(x, c, ctx, c_ctx, w_mod, b_mod, norm_mix, w_in, q_norm_a, k_norm_a, rpb_b, conv_c,
              a_log_c, dt_bias_c, o_norm_c, w_branch, w_out, norm_ffn, w_ffn1, w_ffn2, norm_final):
    n_tok = x.shape[1]
    cos, sin = axial_rope(n_tok)
    xc = ctx
    for l in range(DEPTH):
        mod = (jax.nn.silu(c) @ w_mod[l] + b_mod[l])[:, None, :]
        mod_c = jax.nn.silu(c_ctx) @ w_mod[l] + b_mod[l]
        params = (norm_mix[l], w_in[l], q_norm_a[l], k_norm_a[l], rpb_b[l], conv_c[l], a_log_c[l],
                  dt_bias_c[l], o_norm_c[l], w_branch[l], w_out[l], norm_ffn[l], w_ffn1[l], w_ffn2[l])
        x, xc = trunk_layer(x, xc, mod, mod_c, params, cos, sin, l < DEPTH - 1)
    return rms_norm(x, norm_final)
```

```python
import functools
import math

import jax
import jax.numpy as jnp
from jax import lax
from jax.experimental import pallas as pl
from jax.experimental.pallas import tpu as pltpu

F32 = jnp.float32
CDT = jnp.bfloat16

GRID_W = 64
HEAD_DIM = 64
A_HEADS = 8
A_KV_HEADS = 2
B_HEADS = 8
WIN_R = 8
WIN_C = 16
C_HEADS = 4
C_DK = 128
C_DV = 128
CONV_K = 5
CHUNK = 64
N_BRANCH = 3
ROPE_BASE = 10000.0
EPS = 1e-6

A_Q = A_HEADS * HEAD_DIM
A_KV = A_KV_HEADS * HEAD_DIM
B_W = B_HEADS * HEAD_DIM
C_K = C_HEADS * C_DK
C_V = C_HEADS * C_DV
N_GATE_DIRS = 2 * C_HEADS

LANES = 128
BF16_SUBLANES = 16
VMEM_LIMIT = 56 * 1024 * 1024

CB = 512
J_QA, J_KV, J_GATE0, J_QB, J_KB, J_VB, J_GDN0, J_Z = 0, 1, 2, 8, 9, 10, 11, 14
N_JBLK = 15
PROJ_OFF = 2
P_QB, P_KB, P_VB, P_GDN0, P_Z = (J_QB - PROJ_OFF, J_KB - PROJ_OFF, J_VB - PROJ_OFF, J_GDN0 - PROJ_OFF,
                                 J_Z - PROJ_OFF)
N_PBLK = N_JBLK - PROJ_OFF


def _cparams(*sem):
    return pltpu.CompilerParams(dimension_semantics=sem, vmem_limit_bytes=VMEM_LIMIT)


def _dot(a, b):
    return jnp.dot(a.astype(CDT), b.astype(CDT), preferred_element_type=F32)


def _dot_nt(a, b):
    return lax.dot_general(a.astype(CDT), b.astype(CDT), (((1,), (1,)), ((), ())), preferred_element_type=F32)


def _dot_tn(a, b):
    return lax.dot_general(a.astype(CDT), b.astype(CDT), (((0,), (0,)), ((), ())), preferred_element_type=F32)


def _sigmoid(x):
    return 1.0 / (1.0 + jnp.exp(-x))


def _silu(x):
    return x * _sigmoid(x)


def _mod_kernel(c_ref, w_ref, b_ref, o_ref):
    o_ref[0] = _dot(_silu(c_ref[...]), w_ref[0]) + b_ref[0]


def _modulation(cc, w_mod, b_mod):
    depth, d, n = w_mod.shape
    tn = 1024
    return pl.pallas_call(
        _mod_kernel,
        grid=(depth, n // tn),
        in_specs=[pl.BlockSpec((8, d), lambda l, j: (0, 0)),
                  pl.BlockSpec((1, d, tn), lambda l, j: (l, 0, j)),
                  pl.BlockSpec((1, 1, tn), lambda l, j: (l, 0, j))],
        out_specs=pl.BlockSpec((1, 8, tn), lambda l, j: (l, 0, j)),
        out_shape=jax.ShapeDtypeStruct((depth, 8, n), F32),
        compiler_params=_cparams("arbitrary", "arbitrary"),
        name="modulation",
    )(cc, w_mod, b_mod.reshape(depth, 1, n))


def _head_norm_rope(xT, c, sa, sb):
    n = xT * lax.rsqrt(jnp.mean(xT * xT, axis=0, keepdims=True) + EPS)
    return n * c + pltpu.roll(n, HEAD_DIM - 1, 0) * sa + pltpu.roll(n, 1, 0) * sb


def _inproj_kernel(x_ref, sh_ref, sc_ref, gain_ref, w_ref, rope_ref, par_ref,
                   qT_ref, k_ref, vT_ref, bg_ref, bgT_ref, proj_ref, h_ref, *, row_mul, row_off, tk):
    b = pl.program_id(0)
    j = pl.program_id(2)
    tm = x_ref.shape[1]

    @pl.when(j == 0)
    def _():
        x = x_ref[0]
        r = lax.rsqrt(jnp.mean(x * x, axis=-1, keepdims=True) + EPS)
        row = b * row_mul + row_off
        sh = sh_ref[pl.ds(row, 1), :]
        sc = sc_ref[pl.ds(row, 1), :]
        h_ref[...] = ((x * r * gain_ref[...]) * (1.0 + sc) + sh).astype(CDT)

    def mm():
        return jnp.dot(h_ref[...], w_ref[0], preferred_element_type=F32)

    @pl.when(j == J_QA)
    def _():
        t = mm().T
        zeros = jnp.zeros((HEAD_DIM, tm), F32)
        for h in range(A_HEADS):
            o = _head_norm_rope(t[HEAD_DIM * h:HEAD_DIM * (h + 1)], rope_ref[0], rope_ref[1], rope_ref[2])
            full = jnp.concatenate([o, zeros] if h < A_HEADS // A_KV_HEADS else [zeros, o], axis=0)
            qT_ref[0, h] = full.astype(CDT)

    @pl.when(j == J_KV)
    def _():
        acc = mm()
        t = acc[:, :A_KV].T
        kT = jnp.concatenate(
            [_head_norm_rope(t[HEAD_DIM * g:HEAD_DIM * (g + 1)], rope_ref[3], rope_ref[4], rope_ref[5])
             for g in range(A_KV_HEADS)], axis=0)
        k_ref[0] = kT.T.astype(CDT)
        vT = acc[:, A_KV:2 * A_KV].T
        for c in range(tm // tk):
            vT_ref[0, c] = vT[:, c * tk:(c + 1) * tk].astype(CDT)
        raw = acc[:, 2 * A_KV:2 * A_KV + LANES]
        lane = lax.broadcasted_iota(jnp.int32, (tm, LANES), 1)
        rowi = lax.broadcasted_iota(jnp.int32, (tm, LANES), 0) % CHUNK
        beta = _sigmoid(raw)
        y = raw + par_ref[1:2, :]
        g = -jnp.exp(par_ref[0:1, :]) * (jnp.maximum(y, 0.0) + jnp.log1p(jnp.exp(-jnp.abs(y))))
        fwd = g
        rev = g
        for s in (1, 2, 4, 8, 16, 32):
            fwd = fwd + jnp.where(rowi >= s, pltpu.roll(fwd, s, 0), 0.0)
            rev = rev + jnp.where(rowi < CHUNK - s, pltpu.roll(rev, tm - s, 0), 0.0)
        is_rev = (lane >= N_GATE_DIRS + C_HEADS) & (lane < 2 * N_GATE_DIRS)
        bg = jnp.where(lane < N_GATE_DIRS, beta, jnp.where(is_rev, rev, fwd))
        bg = jnp.where(lane < 2 * N_GATE_DIRS, bg, 0.0)
        bg_ref[0] = bg
        bgT_ref[0] = bg.T[:2 * N_GATE_DIRS]

    @pl.when(j == J_QB)
    def _():
        proj_ref[0] = (mm() * HEAD_DIM ** -0.5).astype(CDT)

    @pl.when((j >= J_GATE0) & (j != J_QB))
    def _():
        proj_ref[0] = mm().astype(CDT)


def _inproj(x, mod_l, gain, w_l, rope, par, *, row_mul, row_off, tm, tk):
    bsz, s, d = x.shape
    kern = functools.partial(_inproj_kernel, row_mul=row_mul, row_off=row_off, tk=tk)
    outs = pl.pallas_call(
        kern,
        grid=(bsz, s // tm, N_JBLK),
        in_specs=[pl.BlockSpec((1, tm, d), lambda b, i, j: (b, i, 0)),
                  pl.BlockSpec((8, d), lambda b, i, j: (0, 0)),
                  pl.BlockSpec((8, d), lambda b, i, j: (0, 1)),
                  pl.BlockSpec((1, d), lambda b, i, j: (0, 0)),
                  pl.BlockSpec((1, d, CB), lambda b, i, j: (0, 0, j)),
                  pl.BlockSpec((6, HEAD_DIM, tm), lambda b, i, j: (0, 0, i)),
                  pl.BlockSpec((8, LANES), lambda b, i, j: (0, 0))],
        out_specs=[pl.BlockSpec((1, A_HEADS, 2 * HEAD_DIM, tm), lambda b, i, j: (b, 0, 0, i)),
                   pl.BlockSpec((1, tm, A_KV), lambda b, i, j: (b, i, 0)),
                   pl.BlockSpec((1, tm // tk, A_KV, tk), lambda b, i, j: (b, i, 0, 0)),
                   pl.BlockSpec((1, tm, LANES), lambda b, i, j: (b, i, 0)),
                   pl.BlockSpec((1, 2 * N_GATE_DIRS, tm), lambda b, i, j: (b, 0, i)),
                   pl.BlockSpec((1, tm, CB), lambda b, i, j: (b, i, jnp.maximum(j - PROJ_OFF, 0)))],
        out_shape=[jax.ShapeDtypeStruct((bsz, A_HEADS, 2 * HEAD_DIM, s), CDT),
                   jax.ShapeDtypeStruct((bsz, s, A_KV), CDT),
                   jax.ShapeDtypeStruct((bsz, s // tk, A_KV, tk), CDT),
                   jax.ShapeDtypeStruct((bsz, s, LANES), F32),
                   jax.ShapeDtypeStruct((bsz, 2 * N_GATE_DIRS, s), F32),
                   jax.ShapeDtypeStruct((bsz, s, N_PBLK * CB), CDT)],
        scratch_shapes=[pltpu.VMEM((tm, d), CDT)],
        compiler_params=_cparams("arbitrary", "arbitrary", "arbitrary"),
        name="inproj",
    )(x, mod_l, mod_l, gain.reshape(1, d), w_l, rope, par)
    return dict(zip(("qT", "k", "vT", "bg", "bgT", "proj"), outs))


def _flash_update(carry, kb, vTb, qh):
    m, l, acc = carry
    s = jnp.dot(kb, qh, preferred_element_type=F32)
    m_new = jnp.maximum(m, jnp.max(s, axis=0, keepdims=True))
    p = jnp.exp(s - m_new)
    alpha = jnp.exp(m - m_new)
    l = alpha * l + jnp.sum(p, axis=0, keepdims=True)
    acc = alpha * acc + jnp.dot(vTb, p.astype(CDT), preferred_element_type=F32)
    return m_new, l, acc


def _gqa_kernel(*refs, n_lat):
    if n_lat:
        q_ref, k_ref, vT_ref, kc_ref, vcT_ref, o_ref = refs
    else:
        q_ref, kc_ref, vcT_ref, o_ref = refs
    tq = q_ref.shape[3]

    def head(h, _):
        g = h // (A_HEADS // A_KV_HEADS)
        row0 = pl.multiple_of(g * HEAD_DIM, HEAD_DIM)
        qh = q_ref[0, h]
        carry = (jnp.full((1, tq), -jnp.inf, F32), jnp.zeros((1, tq), F32), jnp.zeros((HEAD_DIM, tq), F32))
        if n_lat:
            tk = vT_ref.shape[3]

            def chunk(c, carry):
                start = pl.multiple_of(c * tk, tk)
                return _flash_update(carry, k_ref[0, pl.ds(start, tk), :],
                                     vT_ref[0, c, pl.ds(row0, HEAD_DIM), :], qh)

            carry = lax.fori_loop(0, n_lat, chunk, carry)
        _, l, acc = _flash_update(carry, kc_ref[0], vcT_ref[0, 0, pl.ds(row0, HEAD_DIM), :], qh)
        o_ref[0, pl.ds(pl.multiple_of(h * HEAD_DIM, HEAD_DIM), HEAD_DIM), :] = (acc / l).astype(o_ref.dtype)
        return 0

    lax.fori_loop(0, A_HEADS, head, 0)


def _gqa(qT, kc, vcT, k=None, vT=None, *, tq):
    bsz, _, _, sq = qT.shape
    lc = kc.shape[1]
    in_specs = [pl.BlockSpec((1, A_HEADS, 2 * HEAD_DIM, tq), lambda b, i: (b, 0, 0, i))]
    args = [qT]
    n_lat = 0
    if k is not None:
        sk = k.shape[1]
        n_lat, tk = vT.shape[1], vT.shape[3]
        in_specs += [pl.BlockSpec((1, sk, A_KV), lambda b, i: (b, 0, 0)),
                     pl.BlockSpec((1, n_lat, A_KV, tk), lambda b, i: (b, 0, 0, 0))]
        args += [k, vT]
    in_specs += [pl.BlockSpec((1, lc, A_KV), lambda b, i: (b, 0, 0)),
                 pl.BlockSpec((1, 1, A_KV, lc), lambda b, i: (b, 0, 0, 0))]
    args += [kc, vcT]
    return pl.pallas_call(
        functools.partial(_gqa_kernel, n_lat=n_lat),
        grid=(bsz, sq // tq),
        in_specs=in_specs,
        out_specs=pl.BlockSpec((1, A_Q, tq), lambda b, i: (b, 0, i)),
        out_shape=jax.ShapeDtypeStruct((bsz, A_Q, sq), CDT),
        compiler_params=_cparams("arbitrary", "arbitrary"),
        name="gqa" if n_lat else "gqa_ctx",
    )(*args)


def _pair_queries(qp):
    lane = lax.broadcasted_iota(jnp.int32, qp.shape, 1)
    zero = jnp.zeros_like(qp)
    return jnp.concatenate([jnp.where(lane < HEAD_DIM, qp, zero), jnp.where(lane >= HEAD_DIM, qp, zero)], axis=0)


def _pair_finish(pv, l):
    n = pv.shape[0] // 2
    pv = pv / l
    lane = lax.broadcasted_iota(jnp.int32, (n, 2 * HEAD_DIM), 1)
    return jnp.where(lane < HEAD_DIM, pv[:n], pv[n:])


def _natten_kernel(q_ref, k_ref, v_ref, kc_ref, vc_ref, bias_ref, o_ref, *, rows_per_step, n_rows):
    i = pl.program_id(1)
    win = WIN_R * GRID_W
    for rr in range(rows_per_step):
        r = i * rows_per_step + rr
        r0 = jnp.clip(r - WIN_R // 2, 0, n_rows - WIN_R)
        start = pl.multiple_of(r0 * GRID_W, GRID_W)
        cfg = r0 - r + (WIN_R - 1)
        outs = []
        for p in range(B_HEADS // 2):
            cols = slice(2 * HEAD_DIM * p, 2 * HEAD_DIM * (p + 1))
            qs = _pair_queries(q_ref[0, GRID_W * rr:GRID_W * (rr + 1), cols])
            s_loc = _dot_nt(qs, k_ref[0, pl.ds(start, win), cols]) + bias_ref[cfg, p]
            s_ctx = _dot_nt(qs, kc_ref[0, :, cols])
            m = jnp.maximum(jnp.max(s_loc, axis=-1, keepdims=True), jnp.max(s_ctx, axis=-1, keepdims=True))
            e_loc = jnp.exp(s_loc - m)
            e_ctx = jnp.exp(s_ctx - m)
            l = jnp.sum(e_loc, axis=-1, keepdims=True) + jnp.sum(e_ctx, axis=-1, keepdims=True)
            pv = _dot(e_loc, v_ref[0, pl.ds(start, win), cols]) + _dot(e_ctx, vc_ref[0, :, cols])
            outs.append(_pair_finish(pv, l))
        o_ref[0, GRID_W * rr:GRID_W * (rr + 1), :] = jnp.concatenate(outs, axis=-1).astype(o_ref.dtype)


def _natten(proj, cproj, bias, *, rows_per_step):
    bsz, s, _ = proj.shape
    lc = cproj.shape[1]
    n_rows = s // GRID_W
    tq = rows_per_step * GRID_W
    whole = lambda blk: (lambda b, i: (b, 0, blk))
    return pl.pallas_call(
        functools.partial(_natten_kernel, rows_per_step=rows_per_step, n_rows=n_rows),
        grid=(bsz, s // tq),
        in_specs=[pl.BlockSpec((1, tq, CB), lambda b, i: (b, i, P_QB)),
                  pl.BlockSpec((1, s, CB), whole(P_KB), pipeline_mode=pl.Buffered(1)),
                  pl.BlockSpec((1, s, CB), whole(P_VB), pipeline_mode=pl.Buffered(1)),
                  pl.BlockSpec((1, lc, CB), whole(P_KB)),
                  pl.BlockSpec((1, lc, CB), whole(P_VB)),
                  pl.BlockSpec(bias.shape, lambda b, i: (0, 0, 0, 0), pipeline_mode=pl.Buffered(1))],
        out_specs=pl.BlockSpec((1, tq, B_W), lambda b, i: (b, i, 0)),
        out_shape=jax.ShapeDtypeStruct((bsz, s, B_W), CDT),
        compiler_params=_cparams("arbitrary", "arbitrary"),
        name="natten",
    )(proj, proj, proj, cproj, cproj, bias)


def _ctx_attn_b_kernel(q_ref, k_ref, v_ref, o_ref):
    outs = []
    for p in range(B_HEADS // 2):
        cols = slice(2 * HEAD_DIM * p, 2 * HEAD_DIM * (p + 1))
        qs = _pair_queries(q_ref[0, :, cols])
        s = _dot_nt(qs, k_ref[0, :, cols])
        e = jnp.exp(s - jnp.max(s, axis=-1, keepdims=True))
        outs.append(_pair_finish(_dot(e, v_ref[0, :, cols]), jnp.sum(e, axis=-1, keepdims=True)))
    o_ref[0] = jnp.concatenate(outs, axis=-1).astype(o_ref.dtype)


def _ctx_attn_b(cproj):
    bsz, lc, _ = cproj.shape
    blk = lambda c: pl.BlockSpec((1, lc, CB), lambda b: (b, 0, c))
    return pl.pallas_call(
        _ctx_attn_b_kernel,
        grid=(bsz,),
        in_specs=[blk(P_QB), blk(P_KB), blk(P_VB)],
        out_specs=pl.BlockSpec((1, lc, B_W), lambda b: (b, 0, 0)),
        out_shape=jax.ShapeDtypeStruct((bsz, lc, B_W), CDT),
        compiler_params=_cparams("arbitrary"),
        name="ctx_attn_b",
    )(cproj, cproj, cproj)


def _natten_bias(rpb):
    h = rpb.shape[0]
    qc = jnp.arange(GRID_W)
    c0 = jnp.clip(qc - WIN_C // 2, 0, GRID_W - WIN_C)
    kc = jnp.arange(GRID_W)
    inwin = (kc[None, :] >= c0[:, None]) & (kc[None, :] < c0[:, None] + WIN_C)
    dc = jnp.clip(kc[None, :] - qc[:, None] + WIN_C - 1, 0, 2 * WIN_C - 2)
    cfg = jnp.arange(WIN_R)
    dr = cfg[:, None] + jnp.arange(WIN_R)[None, :]
    tab = rpb[:, dr[:, :, None, None], dc[None, None, :, :]]
    tab = jnp.where(inwin[None, None, None], tab, -1e30)
    tab = tab.transpose(1, 0, 3, 2, 4).reshape(WIN_R, h // 2, 2 * GRID_W, WIN_R * GRID_W)
    return tab.astype(F32)


def _gdn_feat_kernel(x_ref, p_ref, n_ref, w_ref, o_ref):
    i = pl.program_id(1)
    c = pl.program_id(2)
    tm = x_ref.shape[1]
    x = x_ref[0].astype(F32)
    prev = jnp.where(i > 0, p_ref[0].astype(F32), 0.0)
    nxt = jnp.where(i < pl.num_programs(1) - 1, n_ref[0].astype(F32), 0.0)
    rowi = lax.broadcasted_iota(jnp.int32, x.shape, 0)
    hp = BF16_SUBLANES
    xm1 = jnp.where(rowi == 0, prev[hp - 1:hp], pltpu.roll(x, 1, 0))
    xm2 = jnp.where(rowi == 0, prev[hp - 2:hp - 1], jnp.where(rowi == 1, prev[hp - 1:hp], pltpu.roll(x, 2, 0)))
    xp1 = jnp.where(rowi == tm - 1, nxt[0:1], pltpu.roll(x, tm - 1, 0))
    xp2 = jnp.where(rowi == tm - 2, nxt[0:1], jnp.where(rowi == tm - 1, nxt[1:2], pltpu.roll(x, tm - 2, 0)))
    w = w_ref[...]
    y = _silu(w[0:1] * xm2 + w[1:2] * xm1 + w[2:3] * x + w[3:4] * xp1 + w[4:5] * xp2)

    @pl.when(c == 2)
    def _():
        o_ref[0] = y.astype(o_ref.dtype)

    @pl.when(c < 2)
    def _():
        scale = jnp.where(c == 0, C_DK ** -0.5, 1.0)
        parts = []
        for h in range(C_HEADS):
            yh = y[:, C_DK * h:C_DK * (h + 1)]
            parts.append(yh * (lax.rsqrt(jnp.sum(yh * yh, axis=-1, keepdims=True) + EPS) * scale))
        o_ref[0] = jnp.concatenate(parts, axis=-1).astype(o_ref.dtype)


def _gdn_features(proj, conv_w, *, tm):
    bsz, s, _ = proj.shape
    hp = BF16_SUBLANES
    nh = s // hp
    return pl.pallas_call(
        _gdn_feat_kernel,
        grid=(bsz, s // tm, 3),
        in_specs=[pl.BlockSpec((1, tm, CB), lambda b, i, c: (b, i, P_GDN0 + c)),
                  pl.BlockSpec((1, hp, CB), lambda b, i, c: (b, jnp.maximum(i * (tm // hp) - 1, 0), P_GDN0 + c)),
                  pl.BlockSpec((1, hp, CB), lambda b, i, c: (b, jnp.minimum((i + 1) * (tm // hp), nh - 1),
                                                             P_GDN0 + c)),
                  pl.BlockSpec((CONV_K, CB), lambda b, i, c: (0, c))],
        out_specs=pl.BlockSpec((1, tm, CB), lambda b, i, c: (b, i, c)),
        out_shape=jax.ShapeDtypeStruct((bsz, s, 3 * CB), CDT),
        compiler_params=_cparams("arbitrary", "arbitrary", "arbitrary"),
        name="gdn_features",
    )(proj, proj, proj, conv_w)


def _gdn_chunk(d, feat, bgc, bgr, s_ref, o_ref, rows):
    ii = lax.broadcasted_iota(jnp.int32, (CHUNK, CHUNK), 0)
    jj = lax.broadcasted_iota(jnp.int32, (CHUNK, CHUNK), 1)
    incl = (ii >= jj) if d == 0 else (ii <= jj)
    strict = (ii > jj) if d == 0 else (ii < jj)
    eye = (ii == jj).astype(F32)
    last = CHUNK - 1 if d == 0 else 0
    for h in range(C_HEADS):
        col = d * C_HEADS + h
        beta = bgc[:, col:col + 1]
        gcol = bgc[:, N_GATE_DIRS + col:N_GATE_DIRS + col + 1]
        grow = bgr[N_GATE_DIRS + col:N_GATE_DIRS + col + 1, :]
        qc = feat[:, C_DK * h:C_DK * (h + 1)].astype(F32)
        kc = feat[:, C_K + C_DK * h:C_K + C_DK * (h + 1)].astype(F32)
        vc = feat[:, 2 * C_K + C_DV * h:2 * C_K + C_DV * (h + 1)].astype(F32)
        kb = kc * beta
        decay = jnp.exp(jnp.where(incl, gcol - grow, -jnp.inf))
        nil = jnp.where(strict, -_dot_nt(kb, kc) * decay, 0.0)
        tinv = eye + nil
        power = nil
        for _ in range(int(math.log2(CHUNK)) - 1):
            power = _dot(power, power)
            tinv = tinv + _dot(tinv, power)
        w_v = _dot(tinv, vc * beta)
        k_cd = _dot(tinv, kb * jnp.exp(gcol))
        glast = gcol[last:last + 1]
        k_dec = kc * jnp.exp(glast - gcol)
        qd = qc * jnp.exp(gcol)
        aqk = _dot_nt(qc, kc) * decay
        state = s_ref[d, h]
        v_new = w_v - _dot(k_cd, state)
        s_ref[d, h] = state * jnp.exp(glast) + _dot_tn(k_dec, v_new)
        o_ref[0, rows, C_DV * h:C_DV * (h + 1)] = _dot(qd, state) + _dot(aqk, v_new)


def _gdn_scan_kernel(ff_ref, fr_ref, bgf_ref, bgr_ref, bgTf_ref, bgTr_ref, s0_ref,
                     of_ref, or_ref, sN_ref, s_ref, *, n_chunks):
    step = pl.program_id(1)

    @pl.when(step == 0)
    def _():
        s_ref[...] = s0_ref[0]

    for d, (f_ref, bg_ref, bgT_ref, o_ref) in enumerate(((ff_ref, bgf_ref, bgTf_ref, of_ref),
                                                         (fr_ref, bgr_ref, bgTr_ref, or_ref))):
        order = range(n_chunks) if d == 0 else range(n_chunks - 1, -1, -1)
        for c in order:
            rows = slice(CHUNK * c, CHUNK * (c + 1))
            _gdn_chunk(d, f_ref[0, rows, :], bg_ref[0, rows, :], bgT_ref[0, :, rows], s_ref, o_ref, rows)

    @pl.when(step == pl.num_programs(1) - 1)
    def _():
        sN_ref[0] = s_ref[...]


def _gdn_scan(feat, bg, bgT, state0, *, n_chunks):
    bsz, s, _ = feat.shape
    tb = n_chunks * CHUNK
    ns = s // tb
    fwd = lambda b, i: (b, i, 0)
    rev = lambda b, i: (b, ns - 1 - i, 0)
    fwdT = lambda b, i: (b, 0, i)
    revT = lambda b, i: (b, 0, ns - 1 - i)
    st = lambda b, i: (b, 0, 0, 0, 0)
    st_shape = (1, 2, C_HEADS, C_DK, C_DV)
    return pl.pallas_call(
        functools.partial(_gdn_scan_kernel, n_chunks=n_chunks),
        grid=(bsz, ns),
        in_specs=[pl.BlockSpec((1, tb, 3 * CB), fwd), pl.BlockSpec((1, tb, 3 * CB), rev),
                  pl.BlockSpec((1, tb, LANES), fwd), pl.BlockSpec((1, tb, LANES), rev),
                  pl.BlockSpec((1, 2 * N_GATE_DIRS, tb), fwdT), pl.BlockSpec((1, 2 * N_GATE_DIRS, tb), revT),
                  pl.BlockSpec(st_shape, st)],
        out_specs=[pl.BlockSpec((1, tb, C_V), fwd), pl.BlockSpec((1, tb, C_V), rev),
                   pl.BlockSpec(st_shape, st)],
        out_shape=[jax.ShapeDtypeStruct((bsz, s, C_V), F32), jax.ShapeDtypeStruct((bsz, s, C_V), F32),
                   jax.ShapeDtypeStruct((bsz,) + st_shape[1:], F32)],
        scratch_shapes=[pltpu.VMEM(st_shape[1:], F32)],
        compiler_params=_cparams("arbitrary", "arbitrary"),
        name="gdn_scan",
    )(feat, feat, bg, bg, bgT, bgT, state0)


def _merge_kernel(x_ref, oaT_ref, ob_ref, of_ref, or_ref, z_ref, g0_ref, g1_ref, g2_ref, onorm_ref,
                  wb_ref, wo_ref, gt_ref, out_ref, *, row_mul, row_off):
    b = pl.program_id(0)
    oc = of_ref[0] + or_ref[0]
    parts = []
    for h in range(C_HEADS):
        oh = oc[:, C_DV * h:C_DV * (h + 1)]
        parts.append(oh * lax.rsqrt(jnp.mean(oh * oh, axis=-1, keepdims=True) + EPS) * onorm_ref[...])
    yc = jnp.concatenate(parts, axis=-1) * _silu(z_ref[0].astype(F32))
    m = (_sigmoid(g0_ref[0].astype(F32)) * _dot_tn(oaT_ref[0], wb_ref[0])
         + _sigmoid(g1_ref[0].astype(F32)) * _dot(ob_ref[0], wb_ref[1])
         + _sigmoid(g2_ref[0].astype(F32)) * _dot(yc, wb_ref[2]))
    gt = gt_ref[pl.ds(b * row_mul + row_off, 1), :]
    out_ref[0] = x_ref[0] + gt * _dot(m, wo_ref[...])


def _merge(x, oaT, ob, of, orr, proj, onorm, wb, wo, mod_l, *, row_mul, row_off, tm):
    bsz, s, d = x.shape
    tok = lambda blk: (lambda b, i: (b, i, blk))
    return pl.pallas_call(
        functools.partial(_merge_kernel, row_mul=row_mul, row_off=row_off),
        grid=(bsz, s // tm),
        in_specs=[pl.BlockSpec((1, tm, d), tok(0)),
                  pl.BlockSpec((1, A_Q, tm), lambda b, i: (b, 0, i)),
                  pl.BlockSpec((1, tm, B_W), tok(0)),
                  pl.BlockSpec((1, tm, C_V), tok(0)),
                  pl.BlockSpec((1, tm, C_V), tok(0)),
                  pl.BlockSpec((1, tm, CB), tok(P_Z)),
                  pl.BlockSpec((1, tm, d), tok(0)),
                  pl.BlockSpec((1, tm, d), tok(1)),
                  pl.BlockSpec((1, tm, d), tok(2)),
                  pl.BlockSpec((1, C_DV), lambda b, i: (0, 0)),
                  pl.BlockSpec(wb.shape, lambda b, i: (0, 0, 0)),
                  pl.BlockSpec(wo.shape, lambda b, i: (0, 0)),
                  pl.BlockSpec((8, d), lambda b, i: (0, 2))],
        out_specs=pl.BlockSpec((1, tm, d), tok(0)),
        out_shape=jax.ShapeDtypeStruct((bsz, s, d), F32),
        compiler_params=_cparams("arbitrary", "arbitrary"),
        name="merge",
    )(x, oaT, ob, of, orr, proj, proj, proj, proj, onorm.reshape(1, C_DV), wb, wo, mod_l)


def _mlp_kernel(x_ref, sh_ref, sc_ref, gt_ref, gain_ref, w1_ref, w2_ref, fin_ref, out_ref, h_ref, acc_ref,
                *, row_mul, row_off, final_norm):
    b = pl.program_id(0)
    f = pl.program_id(2)
    row = b * row_mul + row_off

    @pl.when(f == 0)
    def _():
        x = x_ref[0]
        r = lax.rsqrt(jnp.mean(x * x, axis=-1, keepdims=True) + EPS)
        h = (x * r * gain_ref[...]) * (1.0 + sc_ref[pl.ds(row, 1), :]) + sh_ref[pl.ds(row, 1), :]
        h_ref[...] = h.astype(CDT)
        acc_ref[...] = jnp.zeros_like(acc_ref)

    a = jnp.maximum(jnp.dot(h_ref[...], w1_ref[...], preferred_element_type=F32), 0.0)
    acc_ref[...] += _dot(a * a, w2_ref[...])

    @pl.when(f == pl.num_programs(2) - 1)
    def _():
        y = x_ref[0] + gt_ref[pl.ds(row, 1), :] * acc_ref[...]
        if final_norm:
            y = y * lax.rsqrt(jnp.mean(y * y, axis=-1, keepdims=True) + EPS) * fin_ref[...]
        out_ref[0] = y


def _mlp(x, mod_l, gain, w1, w2, fin, *, row_mul, row_off, tm, tf, final_norm):
    bsz, s, d = x.shape
    dff = w1.shape[1]
    return pl.pallas_call(
        functools.partial(_mlp_kernel, row_mul=row_mul, row_off=row_off, final_norm=final_norm),
        grid=(bsz, s // tm, dff // tf),
        in_specs=[pl.BlockSpec((1, tm, d), lambda b, i, f: (b, i, 0)),
                  pl.BlockSpec((8, d), lambda b, i, f: (0, 3)),
                  pl.BlockSpec((8, d), lambda b, i, f: (0, 4)),
                  pl.BlockSpec((8, d), lambda b, i, f: (0, 5)),
                  pl.BlockSpec((1, d), lambda b, i, f: (0, 0)),
                  pl.BlockSpec((d, tf), lambda b, i, f: (0, f)),
                  pl.BlockSpec((tf, d), lambda b, i, f: (f, 0)),
                  pl.BlockSpec((1, d), lambda b, i, f: (0, 0))],
        out_specs=pl.BlockSpec((1, tm, d), lambda b, i, f: (b, i, 0)),
        out_shape=jax.ShapeDtypeStruct((bsz, s, d), F32),
        scratch_shapes=[pltpu.VMEM((tm, d), CDT), pltpu.VMEM((tm, d), F32)],
        compiler_params=_cparams("arbitrary", "arbitrary", "arbitrary"),
        name="mlp",
    )(x, mod_l, mod_l, mod_l, gain.reshape(1, d), w1, w2, fin.reshape(1, d))


def _reorder_w_in(w_in):
    o = 0
    seg = {}
    for name, width in (("qa", A_Q), ("ka", A_KV), ("va", A_KV), ("qb", B_W), ("kb", B_W), ("vb", B_W),
                        ("qkv", 2 * C_K + C_V), ("z", C_V), ("beta", N_GATE_DIRS), ("a", N_GATE_DIRS),
                        ("gate", None)):
        width = w_in.shape[-1] - o if width is None else width
        seg[name] = w_in[..., o:o + width]
        o += width
    pad = jnp.zeros(w_in.shape[:-1] + (CB - 2 * A_KV - 2 * N_GATE_DIRS,), w_in.dtype)
    cols = [seg["qa"], seg["ka"], seg["va"], seg["beta"], seg["a"], pad, seg["gate"], seg["qb"], seg["kb"],
            seg["vb"], seg["qkv"], seg["z"]]
    return jnp.concatenate(cols, axis=-1).astype(CDT)


def _rope_tables(n_tok, gain_q, gain_k, rotate):
    if rotate:
        t = jnp.arange(n_tok)
        row = (t // GRID_W).astype(F32)
        col = (t % GRID_W).astype(F32)
        axis_dim = HEAD_DIM // 2
        freqs = ROPE_BASE ** (-jnp.arange(0, axis_dim, 2, dtype=F32) / axis_dim)
        ang = jnp.concatenate([row[:, None] * freqs, col[:, None] * freqs], axis=-1)
        cos = jnp.repeat(jnp.cos(ang), 2, axis=-1).T
        sin = jnp.repeat(jnp.sin(ang), 2, axis=-1).T
    else:
        cos = jnp.ones((HEAD_DIM, n_tok), F32)
        sin = jnp.zeros((HEAD_DIM, n_tok), F32)
    even = (jnp.arange(HEAD_DIM) % 2 == 0)[:, None]

    def tables(g, scale):
        g = g.astype(F32) * scale
        return [cos * g[:, None],
                jnp.where(even, -sin * jnp.roll(g, -1)[:, None], 0.0),
                jnp.where(even, 0.0, sin * jnp.roll(g, 1)[:, None])]

    return jnp.stack(tables(gain_q, HEAD_DIM ** -0.5) + tables(gain_k, 1.0))


def _decay_params(a_log, dt_bias):
    row = lambda v: jnp.zeros((LANES,), F32).at[N_GATE_DIRS:2 * N_GATE_DIRS].set(v.reshape(-1).astype(F32))
    return jnp.zeros((8, LANES), F32).at[0].set(row(a_log)).at[1].set(row(dt_bias))


def kernel(x, c, ctx, c_ctx, w_mod, b_mod, norm_mix, w_in, q_norm_a, k_norm_a, rpb_b, conv_c, a_log_c, dt_bias_c,
           o_norm_c, w_branch, w_out, norm_ffn, w_ffn1, w_ffn2, norm_final):
    bsz, s, d = x.shape
    lc = ctx.shape[1]
    depth = w_mod.shape[0]
    assert bsz < 8 and s % 1024 == 0 and lc == 256 and d % CB == 0

    tm = 1024
    tk = 512
    cc = jnp.zeros((8, d), F32).at[:bsz].set(c).at[bsz].set(c_ctx)
    mod = _modulation(cc, w_mod, b_mod)
    w_in_r = _reorder_w_in(w_in)
    wb = w_branch.astype(CDT)
    wo = w_out.astype(CDT)
    w1 = w_ffn1.astype(CDT)
    w2 = w_ffn2.astype(CDT)
    lat = dict(row_mul=1, row_off=0)
    con = dict(row_mul=0, row_off=bsz)
    zero_state = jnp.zeros((bsz, 2, C_HEADS, C_DK, C_DV), F32)

    xc = ctx
    for l in range(depth):
        need_ctx = l < depth - 1
        par = _decay_params(a_log_c[l], dt_bias_c[l])
        rope_l = _rope_tables(s, q_norm_a[l], k_norm_a[l], True)
        rope_c = _rope_tables(lc, q_norm_a[l], k_norm_a[l], False)
        bias = _natten_bias(rpb_b[l])
        conv_w = conv_c[l].astype(F32)

        P = _inproj(x, mod[l], norm_mix[l], w_in_r[l:l + 1], rope_l, par, tm=tm, tk=tk, **lat)
        Pc = _inproj(xc, mod[l], norm_mix[l], w_in_r[l:l + 1], rope_c, par, tm=lc, tk=lc, **con)

        o_aT = _gqa(P["qT"], Pc["k"], Pc["vT"], P["k"], P["vT"], tq=512)
        o_b = _natten(P["proj"], Pc["proj"], bias, rows_per_step=8)
        featc = _gdn_features(Pc["proj"], conv_w, tm=lc)
        feat = _gdn_features(P["proj"], conv_w, tm=512)
        oc_f, oc_r, state = _gdn_scan(featc, Pc["bg"], Pc["bgT"], zero_state, n_chunks=2)
        o_f, o_r, _ = _gdn_scan(feat, P["bg"], P["bgT"], state, n_chunks=2)

        last = l == depth - 1
        x = _merge(x, o_aT, o_b, o_f, o_r, P["proj"], o_norm_c[l], wb[l], wo[l], mod[l], tm=tm, **lat)
        x = _mlp(x, mod[l], norm_ffn[l], w1[l], w2[l], norm_final, tm=tm, tf=1024, final_norm=last, **lat)
        if need_ctx:
            oc_aT = _gqa(Pc["qT"], Pc["k"], Pc["vT"], tq=lc)
            oc_b = _ctx_attn_b(Pc["proj"])
            xc = _merge(xc, oc_aT, oc_b, oc_f, oc_r, Pc["proj"], o_norm_c[l], wb[l], wo[l], mod[l], tm=lc, **con)
            xc = _mlp(xc, mod[l], norm_ffn[l], w1[l], w2[l], norm_final, tm=lc, tf=1024, final_norm=False, **con)
    return x
```

```python
import functools
import math

import jax
import jax.numpy as jnp
from jax import lax
from jax.experimental import pallas as pl
from jax.experimental.pallas import tpu as pltpu

F32 = jnp.float32
CDT = jnp.bfloat16

GRID_W = 64
HEAD_DIM = 64
A_HEADS = 8
A_KV_HEADS = 2
B_HEADS = 8
WIN_R = 8
WIN_C = 16
C_HEADS = 4
C_DK = 128
C_DV = 128
CONV_K = 5
CHUNK = 64
N_BRANCH = 3
ROPE_BASE = 10000.0
EPS = 1e-6

A_Q = A_HEADS * HEAD_DIM
A_KV = A_KV_HEADS * HEAD_DIM
B_W = B_HEADS * HEAD_DIM
C_K = C_HEADS * C_DK
C_V = C_HEADS * C_DV
N_GATE_DIRS = 2 * C_HEADS

LANES = 128
BF16_SUBLANES = 16
VMEM_LIMIT = 56 * 1024 * 1024

V_ROWS = HEAD_DIM + BF16_SUBLANES
MAX_UNSHIFTED_LOG2_SCORE = 100.0

CB = 512
J_QA, J_KV, J_GATE0, J_QB, J_KB, J_VB, J_GDN0, J_Z = 0, 1, 2, 8, 9, 10, 11, 14
N_JBLK = 15
PROJ_OFF = 2
P_QB, P_KB, P_VB, P_GDN0, P_Z = (J_QB - PROJ_OFF, J_KB - PROJ_OFF, J_VB - PROJ_OFF, J_GDN0 - PROJ_OFF,
                                 J_Z - PROJ_OFF)
N_PBLK = N_JBLK - PROJ_OFF


def _cparams(*sem):
    return pltpu.CompilerParams(dimension_semantics=sem, vmem_limit_bytes=VMEM_LIMIT)


def _dot(a, b):
    return jnp.dot(a.astype(CDT), b.astype(CDT), preferred_element_type=F32)


def _dot_nt(a, b):
    return lax.dot_general(a.astype(CDT), b.astype(CDT), (((1,), (1,)), ((), ())), preferred_element_type=F32)


def _dot_tn(a, b):
    return lax.dot_general(a.astype(CDT), b.astype(CDT), (((0,), (0,)), ((), ())), preferred_element_type=F32)


def _sigmoid(x):
    return 1.0 / (1.0 + jnp.exp(-x))


def _silu(x):
    return x * _sigmoid(x)


def _mod_kernel(c_ref, w_ref, b_ref, o_ref):
    o_ref[0] = _dot(_silu(c_ref[...]), w_ref[0]) + b_ref[0]


def _modulation(cc, w_mod, b_mod):
    depth, d, n = w_mod.shape
    tn = 1024
    return pl.pallas_call(
        _mod_kernel,
        grid=(depth, n // tn),
        in_specs=[pl.BlockSpec((8, d), lambda l, j: (0, 0)),
                  pl.BlockSpec((1, d, tn), lambda l, j: (l, 0, j)),
                  pl.BlockSpec((1, 1, tn), lambda l, j: (l, 0, j))],
        out_specs=pl.BlockSpec((1, 8, tn), lambda l, j: (l, 0, j)),
        out_shape=jax.ShapeDtypeStruct((depth, 8, n), F32),
        compiler_params=_cparams("arbitrary", "arbitrary"),
        name="modulation",
    )(cc, w_mod, b_mod.reshape(depth, 1, n))


def _head_norm_rope(xT, c, sa, sb):
    n = xT * lax.rsqrt(jnp.mean(xT * xT, axis=0, keepdims=True) + EPS)
    return n * c + pltpu.roll(n, HEAD_DIM - 1, 0) * sa + pltpu.roll(n, 1, 0) * sb


def _inproj_kernel(x_ref, sh_ref, sc_ref, gain_ref, w_ref, rope_ref, par_ref,
                   qT_ref, k_ref, vT_ref, bg_ref, bgT_ref, proj_ref, h_ref, *, row_mul, row_off, tk):
    b = pl.program_id(0)
    j = pl.program_id(2)
    tm = x_ref.shape[1]

    @pl.when(j == 0)
    def _():
        x = x_ref[0]
        r = lax.rsqrt(jnp.mean(x * x, axis=-1, keepdims=True) + EPS)
        row = b * row_mul + row_off
        sh = sh_ref[pl.ds(row, 1), :]
        sc = sc_ref[pl.ds(row, 1), :]
        h_ref[...] = ((x * r * gain_ref[...]) * (1.0 + sc) + sh).astype(CDT)

    def mm():
        return jnp.dot(h_ref[...], w_ref[0], preferred_element_type=F32)

    @pl.when(j == J_QA)
    def _():
        t = mm().T
        zeros = jnp.zeros((HEAD_DIM, tm), F32)
        for h in range(A_HEADS):
            o = _head_norm_rope(t[HEAD_DIM * h:HEAD_DIM * (h + 1)], rope_ref[0], rope_ref[1], rope_ref[2])
            full = jnp.concatenate([o, zeros] if h < A_HEADS // A_KV_HEADS else [zeros, o], axis=0)
            qT_ref[0, h] = full.astype(CDT)

    @pl.when(j == J_KV)
    def _():
        acc = mm()
        t = acc[:, :A_KV].T
        kT = jnp.concatenate(
            [_head_norm_rope(t[HEAD_DIM * g:HEAD_DIM * (g + 1)], rope_ref[3], rope_ref[4], rope_ref[5])
             for g in range(A_KV_HEADS)], axis=0)
        k_ref[0] = kT.T.astype(CDT)
        vT = acc[:, A_KV:2 * A_KV].T
        ones_pad = (lax.broadcasted_iota(jnp.int32, (V_ROWS - HEAD_DIM, tm), 0) == 0).astype(F32)
        vT = jnp.concatenate([piece for g in range(A_KV_HEADS)
                              for piece in (vT[HEAD_DIM * g:HEAD_DIM * (g + 1)], ones_pad)], axis=0)
        for c in range(tm // tk):
            vT_ref[0, c] = vT[:, c * tk:(c + 1) * tk].astype(CDT)
        raw = acc[:, 2 * A_KV:2 * A_KV + LANES]
        lane = lax.broadcasted_iota(jnp.int32, (tm, LANES), 1)
        rowi = lax.broadcasted_iota(jnp.int32, (tm, LANES), 0) % CHUNK
        beta = _sigmoid(raw)
        y = raw + par_ref[1:2, :]
        g = -jnp.exp(par_ref[0:1, :]) * (jnp.maximum(y, 0.0) + jnp.log1p(jnp.exp(-jnp.abs(y))))
        fwd = g
        rev = g
        for s in (1, 2, 4, 8, 16, 32):
            fwd = fwd + jnp.where(rowi >= s, pltpu.roll(fwd, s, 0), 0.0)
            rev = rev + jnp.where(rowi < CHUNK - s, pltpu.roll(rev, tm - s, 0), 0.0)
        is_rev = (lane >= N_GATE_DIRS + C_HEADS) & (lane < 2 * N_GATE_DIRS)
        bg = jnp.where(lane < N_GATE_DIRS, beta, jnp.where(is_rev, rev, fwd))
        bg = jnp.where(lane < 2 * N_GATE_DIRS, bg, 0.0)
        bg_ref[0] = bg
        bgT_ref[0] = bg.T[:2 * N_GATE_DIRS]

    @pl.when(j == J_QB)
    def _():
        proj_ref[0] = (mm() * HEAD_DIM ** -0.5).astype(CDT)

    @pl.when((j >= J_GATE0) & (j != J_QB))
    def _():
        proj_ref[0] = mm().astype(CDT)


def _inproj(x, mod_l, gain, w_l, rope, par, *, row_mul, row_off, tm, tk):
    bsz, s, d = x.shape
    kern = functools.partial(_inproj_kernel, row_mul=row_mul, row_off=row_off, tk=tk)
    outs = pl.pallas_call(
        kern,
        grid=(bsz, s // tm, N_JBLK),
        in_specs=[pl.BlockSpec((1, tm, d), lambda b, i, j: (b, i, 0)),
                  pl.BlockSpec((8, d), lambda b, i, j: (0, 0)),
                  pl.BlockSpec((8, d), lambda b, i, j: (0, 1)),
                  pl.BlockSpec((1, d), lambda b, i, j: (0, 0)),
                  pl.BlockSpec((1, d, CB), lambda b, i, j: (0, 0, j)),
                  pl.BlockSpec((6, HEAD_DIM, tm), lambda b, i, j: (0, 0, i)),
                  pl.BlockSpec((8, LANES), lambda b, i, j: (0, 0))],
        out_specs=[pl.BlockSpec((1, A_HEADS, 2 * HEAD_DIM, tm), lambda b, i, j: (b, 0, 0, i)),
                   pl.BlockSpec((1, tm, A_KV), lambda b, i, j: (b, i, 0)),
                   pl.BlockSpec((1, tm // tk, A_KV_HEADS * V_ROWS, tk), lambda b, i, j: (b, i, 0, 0)),
                   pl.BlockSpec((1, tm, LANES), lambda b, i, j: (b, i, 0)),
                   pl.BlockSpec((1, 2 * N_GATE_DIRS, tm), lambda b, i, j: (b, 0, i)),
                   pl.BlockSpec((1, tm, CB), lambda b, i, j: (b, i, jnp.maximum(j - PROJ_OFF, 0)))],
        out_shape=[jax.ShapeDtypeStruct((bsz, A_HEADS, 2 * HEAD_DIM, s), CDT),
                   jax.ShapeDtypeStruct((bsz, s, A_KV), CDT),
                   jax.ShapeDtypeStruct((bsz, s // tk, A_KV_HEADS * V_ROWS, tk), CDT),
                   jax.ShapeDtypeStruct((bsz, s, LANES), F32),
                   jax.ShapeDtypeStruct((bsz, 2 * N_GATE_DIRS, s), F32),
                   jax.ShapeDtypeStruct((bsz, s, N_PBLK * CB), CDT)],
        scratch_shapes=[pltpu.VMEM((tm, d), CDT)],
        compiler_params=_cparams("arbitrary", "arbitrary", "arbitrary"),
        name="inproj",
    )(x, mod_l, mod_l, gain.reshape(1, d), w_l, rope, par)
    return dict(zip(("qT", "k", "vT", "bg", "bgT", "proj"), outs))


def _gqa_kernel(*refs, n_lat, online):
    if n_lat:
        q_ref, k_ref, vT_ref, kc_ref, vcT_ref, o_ref, m_ref, acc_ref = refs
    else:
        q_ref, kc_ref, vcT_ref, o_ref, m_ref, acc_ref = refs
    group = A_HEADS // A_KV_HEADS
    for g in range(A_KV_HEADS):
        rows = slice(g * V_ROWS, (g + 1) * V_ROWS)
        acc_ref[...] = jnp.zeros(acc_ref.shape, F32)
        if online:
            m_ref[...] = jnp.full(m_ref.shape, -jnp.inf, F32)

        def update(hh, s, vTb):
            if online:
                m_old = m_ref[hh]
                m_new = jnp.maximum(m_old, jnp.max(s, axis=0, keepdims=True))
                m_ref[hh] = m_new
                p = jnp.exp2(s - m_new).astype(CDT)
                acc_ref[hh] = jnp.exp2(m_old - m_new) * acc_ref[hh] + jnp.dot(vTb, p, preferred_element_type=F32)
            else:
                acc_ref[hh] += jnp.dot(vTb, jnp.exp2(s).astype(CDT), preferred_element_type=F32)

        def heads(kb, vTb):
            scores = lambda hh: jnp.dot(kb, q_ref[0, g * group + hh], preferred_element_type=F32)
            s_next = scores(0)
            for hh in range(group):
                s = s_next
                if hh + 1 < group:
                    s_next = scores(hh + 1)
                update(hh, s, vTb)

        if n_lat:
            tk = vT_ref.shape[3]

            def chunk(c, carry):
                start = pl.multiple_of(c * tk, tk)
                heads(k_ref[0, pl.ds(start, tk), :], vT_ref[0, c, rows, :])
                return carry

            lax.fori_loop(0, n_lat, chunk, 0)
        heads(kc_ref[0], vcT_ref[0, 0, rows, :])
        for hh in range(group):
            h = g * group + hh
            a = acc_ref[hh]
            o_ref[0, HEAD_DIM * h:HEAD_DIM * (h + 1), :] = (a[:HEAD_DIM] / a[HEAD_DIM:HEAD_DIM + 1]).astype(o_ref.dtype)


def _gqa_call(qT, kc, vcT, k, vT, *, tq, online):
    bsz, _, _, sq = qT.shape
    lc = kc.shape[1]
    vr = vcT.shape[2]
    in_specs = [pl.BlockSpec((1, A_HEADS, 2 * HEAD_DIM, tq), lambda b, i: (b, 0, 0, i))]
    args = [qT]
    n_lat = 0
    if k is not None:
        sk = k.shape[1]
        n_lat, tk = vT.shape[1], vT.shape[3]
        in_specs += [pl.BlockSpec((1, sk, A_KV), lambda b, i: (b, 0, 0)),
                     pl.BlockSpec((1, n_lat, vr, tk), lambda b, i: (b, 0, 0, 0))]
        args += [k, vT]
    in_specs += [pl.BlockSpec((1, lc, A_KV), lambda b, i: (b, 0, 0)),
                 pl.BlockSpec((1, 1, vr, lc), lambda b, i: (b, 0, 0, 0))]
    args += [kc, vcT]
    group = A_HEADS // A_KV_HEADS
    return pl.pallas_call(
        functools.partial(_gqa_kernel, n_lat=n_lat, online=online),
        grid=(bsz, sq // tq),
        in_specs=in_specs,
        out_specs=pl.BlockSpec((1, A_Q, tq), lambda b, i: (b, 0, i)),
        out_shape=jax.ShapeDtypeStruct((bsz, A_Q, sq), CDT),
        scratch_shapes=[pltpu.VMEM((group, 1, tq), F32), pltpu.VMEM((group, V_ROWS, tq), F32)],
        compiler_params=_cparams("arbitrary", "arbitrary"),
        name=("gqa" if n_lat else "gqa_ctx") + ("_online" if online else ""),
    )(*args)


def _gqa(qT, kc, vcT, k, vT, gain_q, gain_k, *, tq):
    bound = (HEAD_DIM ** 0.5 * math.log2(math.e)) * jnp.max(jnp.abs(gain_q)) * jnp.max(jnp.abs(gain_k))
    return lax.cond(bound < MAX_UNSHIFTED_LOG2_SCORE,
                    functools.partial(_gqa_call, tq=tq, online=False),
                    functools.partial(_gqa_call, tq=tq, online=True),
                    qT, kc, vcT, k, vT)


def _pair_queries(qp):
    lane = lax.broadcasted_iota(jnp.int32, qp.shape, 1)
    zero = jnp.zeros_like(qp)
    return jnp.concatenate([jnp.where(lane < HEAD_DIM, qp, zero), jnp.where(lane >= HEAD_DIM, qp, zero)], axis=0)


def _pair_finish(pv, l):
    n = pv.shape[0] // 2
    pv = pv / l
    lane = lax.broadcasted_iota(jnp.int32, (n, 2 * HEAD_DIM), 1)
    return jnp.where(lane < HEAD_DIM, pv[:n], pv[n:])


def _natten_kernel(q_ref, k_ref, v_ref, kc_ref, vc_ref, bias_ref, o_ref, *, rows_per_step, n_rows):
    i = pl.program_id(1)
    win = WIN_R * GRID_W
    for rr in range(rows_per_step):
        r = i * rows_per_step + rr
        r0 = jnp.clip(r - WIN_R // 2, 0, n_rows - WIN_R)
        start = pl.multiple_of(r0 * GRID_W, GRID_W)
        cfg = r0 - r + (WIN_R - 1)
        outs = []
        for p in range(B_HEADS // 2):
            cols = slice(2 * HEAD_DIM * p, 2 * HEAD_DIM * (p + 1))
            qs = _pair_queries(q_ref[0, GRID_W * rr:GRID_W * (rr + 1), cols])
            s_loc = _dot_nt(qs, k_ref[0, pl.ds(start, win), cols]) + bias_ref[cfg, p]
            s_ctx = _dot_nt(qs, kc_ref[0, :, cols])
            m = jnp.maximum(jnp.max(s_loc, axis=-1, keepdims=True), jnp.max(s_ctx, axis=-1, keepdims=True))
            e_loc = jnp.exp(s_loc - m)
            e_ctx = jnp.exp(s_ctx - m)
            l = jnp.sum(e_loc, axis=-1, keepdims=True) + jnp.sum(e_ctx, axis=-1, keepdims=True)
            pv = _dot(e_loc, v_ref[0, pl.ds(start, win), cols]) + _dot(e_ctx, vc_ref[0, :, cols])
            outs.append(_pair_finish(pv, l))
        o_ref[0, GRID_W * rr:GRID_W * (rr + 1), :] = jnp.concatenate(outs, axis=-1).astype(o_ref.dtype)


def _natten(proj, cproj, bias, *, rows_per_step):
    bsz, s, _ = proj.shape
    lc = cproj.shape[1]
    n_rows = s // GRID_W
    tq = rows_per_step * GRID_W
    whole = lambda blk: (lambda b, i: (b, 0, blk))
    return pl.pallas_call(
        functools.partial(_natten_kernel, rows_per_step=rows_per_step, n_rows=n_rows),
        grid=(bsz, s // tq),
        in_specs=[pl.BlockSpec((1, tq, CB), lambda b, i: (b, i, P_QB)),
                  pl.BlockSpec((1, s, CB), whole(P_KB), pipeline_mode=pl.Buffered(1)),
                  pl.BlockSpec((1, s, CB), whole(P_VB), pipeline_mode=pl.Buffered(1)),
                  pl.BlockSpec((1, lc, CB), whole(P_KB)),
                  pl.BlockSpec((1, lc, CB), whole(P_VB)),
                  pl.BlockSpec(bias.shape, lambda b, i: (0, 0, 0, 0), pipeline_mode=pl.Buffered(1))],
        out_specs=pl.BlockSpec((1, tq, B_W), lambda b, i: (b, i, 0)),
        out_shape=jax.ShapeDtypeStruct((bsz, s, B_W), CDT),
        compiler_params=_cparams("arbitrary", "arbitrary"),
        name="natten",
    )(proj, proj, proj, cproj, cproj, bias)


def _ctx_attn_b_kernel(q_ref, k_ref, v_ref, o_ref):
    outs = []
    for p in range(B_HEADS // 2):
        cols = slice(2 * HEAD_DIM * p, 2 * HEAD_DIM * (p + 1))
        qs = _pair_queries(q_ref[0, :, cols])
        s = _dot_nt(qs, k_ref[0, :, cols])
        e = jnp.exp(s - jnp.max(s, axis=-1, keepdims=True))
        outs.append(_pair_finish(_dot(e, v_ref[0, :, cols]), jnp.sum(e, axis=-1, keepdims=True)))
    o_ref[0] = jnp.concatenate(outs, axis=-1).astype(o_ref.dtype)


def _ctx_attn_b(cproj):
    bsz, lc, _ = cproj.shape
    blk = lambda c: pl.BlockSpec((1, lc, CB), lambda b: (b, 0, c))
    return pl.pallas_call(
        _ctx_attn_b_kernel,
        grid=(bsz,),
        in_specs=[blk(P_QB), blk(P_KB), blk(P_VB)],
        out_specs=pl.BlockSpec((1, lc, B_W), lambda b: (b, 0, 0)),
        out_shape=jax.ShapeDtypeStruct((bsz, lc, B_W), CDT),
        compiler_params=_cparams("arbitrary"),
        name="ctx_attn_b",
    )(cproj, cproj, cproj)


def _natten_bias(rpb):
    h = rpb.shape[0]
    nd = 2 * WIN_C - 1
    lead = GRID_W - WIN_C
    w = jnp.pad(rpb.astype(F32), ((0, 0), (0, 0), (lead, 2 * GRID_W - lead - nd)))
    t = jnp.tile(w, (1, 1, GRID_W))[..., :GRID_W * (2 * GRID_W - 1)]
    toe = t.reshape(h, 2 * WIN_R - 1, GRID_W, 2 * GRID_W - 1)[..., GRID_W - 1:]
    qc = jnp.arange(GRID_W)
    c0 = jnp.clip(qc - WIN_C // 2, 0, GRID_W - WIN_C)
    inwin = (qc[None, :] >= c0[:, None]) & (qc[None, :] < c0[:, None] + WIN_C)
    toe = jnp.where(inwin, toe, -1e30)
    tab = jnp.stack([toe[:, cfg:cfg + WIN_R] for cfg in range(WIN_R)])
    return tab.transpose(0, 1, 3, 2, 4).reshape(WIN_R, h // 2, 2 * GRID_W, WIN_R * GRID_W)


def _gdn_feat_kernel(x_ref, p_ref, n_ref, w_ref, o_ref):
    i = pl.program_id(1)
    c = pl.program_id(2)
    tm = x_ref.shape[1]
    x = x_ref[0].astype(F32)
    prev = jnp.where(i > 0, p_ref[0].astype(F32), 0.0)
    nxt = jnp.where(i < pl.num_programs(1) - 1, n_ref[0].astype(F32), 0.0)
    rowi = lax.broadcasted_iota(jnp.int32, x.shape, 0)
    hp = BF16_SUBLANES
    xm1 = jnp.where(rowi == 0, prev[hp - 1:hp], pltpu.roll(x, 1, 0))
    xm2 = jnp.where(rowi == 0, prev[hp - 2:hp - 1], jnp.where(rowi == 1, prev[hp - 1:hp], pltpu.roll(x, 2, 0)))
    xp1 = jnp.where(rowi == tm - 1, nxt[0:1], pltpu.roll(x, tm - 1, 0))
    xp2 = jnp.where(rowi == tm - 2, nxt[0:1], jnp.where(rowi == tm - 1, nxt[1:2], pltpu.roll(x, tm - 2, 0)))
    w = w_ref[...]
    y = _silu(w[0:1] * xm2 + w[1:2] * xm1 + w[2:3] * x + w[3:4] * xp1 + w[4:5] * xp2)

    @pl.when(c == 2)
    def _():
        o_ref[0] = y.astype(o_ref.dtype)

    @pl.when(c < 2)
    def _():
        scale = jnp.where(c == 0, C_DK ** -0.5, 1.0)
        parts = []
        for h in range(C_HEADS):
            yh = y[:, C_DK * h:C_DK * (h + 1)]
            parts.append(yh * (lax.rsqrt(jnp.sum(yh * yh, axis=-1, keepdims=True) + EPS) * scale))
        o_ref[0] = jnp.concatenate(parts, axis=-1).astype(o_ref.dtype)


def _gdn_features(proj, conv_w, *, tm):
    bsz, s, _ = proj.shape
    hp = BF16_SUBLANES
    nh = s // hp
    return pl.pallas_call(
        _gdn_feat_kernel,
        grid=(bsz, s // tm, 3),
        in_specs=[pl.BlockSpec((1, tm, CB), lambda b, i, c: (b, i, P_GDN0 + c)),
                  pl.BlockSpec((1, hp, CB), lambda b, i, c: (b, jnp.maximum(i * (tm // hp) - 1, 0), P_GDN0 + c)),
                  pl.BlockSpec((1, hp, CB), lambda b, i, c: (b, jnp.minimum((i + 1) * (tm // hp), nh - 1),
                                                             P_GDN0 + c)),
                  pl.BlockSpec((CONV_K, CB), lambda b, i, c: (0, c))],
        out_specs=pl.BlockSpec((1, tm, CB), lambda b, i, c: (b, i, c)),
        out_shape=jax.ShapeDtypeStruct((bsz, s, 3 * CB), CDT),
        compiler_params=_cparams("arbitrary", "arbitrary", "arbitrary"),
        name="gdn_features",
    )(proj, proj, proj, conv_w)


def _block_diag(x, n, blk):
    lane_blk = lax.broadcasted_iota(jnp.int32, (1, n * blk), 1) // blk
    return jnp.concatenate([jnp.where(lane_blk == h, x, 0.0) for h in range(n)], axis=0)


def _gdn_prepare(d, feat, bgc, bgr):
    nh = C_HEADS
    col0 = d * nh
    last = CHUNK - 1 if d == 0 else 0
    q = feat[:, :C_K].astype(F32)
    k = feat[:, C_K:2 * C_K].astype(F32)
    v = feat[:, 2 * C_K:].astype(F32)

    def wide(c0, width):
        return jnp.concatenate([jnp.broadcast_to(bgc[:, c0 + h:c0 + h + 1], (CHUNK, width)) for h in range(nh)],
                               axis=1)

    beta = wide(col0, C_DK)
    gcol = wide(N_GATE_DIRS + col0, C_DK)
    gcol_c = wide(N_GATE_DIRS + col0, CHUNK)
    grow_c = jnp.concatenate([bgr[N_GATE_DIRS + col0 + h:N_GATE_DIRS + col0 + h + 1, :] for h in range(nh)], axis=1)
    glast = gcol[last:last + 1, :]
    egc = jnp.exp(gcol)
    kb = k * beta
    ii = lax.broadcasted_iota(jnp.int32, (CHUNK, nh * CHUNK), 0)
    jj = lax.broadcasted_iota(jnp.int32, (CHUNK, nh * CHUNK), 1) % CHUNK
    incl = (ii >= jj) if d == 0 else (ii <= jj)
    strict = (ii > jj) if d == 0 else (ii < jj)
    r = _dot_nt(jnp.concatenate([kb, q], axis=0), _block_diag(k, nh, C_DK))
    decay = jnp.exp(jnp.where(incl, gcol_c - grow_c, -jnp.inf))
    nil = jnp.where(strict, -r[:CHUNK] * decay, 0.0)
    rhs_w = jnp.concatenate(
        [jnp.concatenate([(v * beta)[:, C_DV * h:C_DV * (h + 1)], (kb * egc)[:, C_DK * h:C_DK * (h + 1)]], axis=1)
         for h in range(nh)], axis=0).astype(CDT)
    k_dec = k * jnp.exp(glast - gcol)
    return dict(
        tinv=jnp.where(ii == jj, 1.0, 0.0) + nil,
        power=nil,
        aqk=r[CHUNK:] * decay,
        rhs_w=rhs_w,
        qd=(q * egc).astype(CDT),
        k_decT=[k_dec[:, C_DK * h:C_DK * (h + 1)].T.astype(CDT) for h in range(nh)],
        eglast=jnp.exp(glast),
    )


def _gdn_scan_kernel(ff_ref, fr_ref, bgf_ref, bgr_ref, bgTf_ref, bgTr_ref, s0_ref,
                     of_ref, or_ref, sN_ref, s_ref, *, n_chunks):
    step = pl.program_id(0)
    bsz = ff_ref.shape[0]
    nh = C_HEADS
    io = ((ff_ref, bgf_ref, bgTf_ref, of_ref), (fr_ref, bgr_ref, bgTr_ref, or_ref))

    @pl.when(step == 0)
    def _():
        s_ref[...] = s0_ref[...]

    units = {}
    for c in range(n_chunks):
        rows = slice(CHUNK * c, CHUNK * (c + 1))
        for b in range(bsz):
            for d in range(2):
                f_ref, bg_ref, bgT_ref, _ = io[d]
                units[b, d, c] = _gdn_prepare(d, f_ref[b, rows, :], bg_ref[b, rows, :], bgT_ref[b, :, rows])
    us = list(units.values())

    bd = lambda x: _block_diag(x, nh, CHUNK)
    n_sq = int(math.log2(CHUNK)) - 1
    for u in us:
        u["power"] = _dot(u["power"], bd(u["power"]))
    for _ in range(n_sq - 1):
        for u in us:
            r = _dot(jnp.concatenate([u["tinv"], u["power"]], axis=0), bd(u["power"]))
            u["tinv"] = u["tinv"] + r[:CHUNK]
            u["power"] = r[CHUNK:]
    for u in us:
        u["tinv"] = u["tinv"] + _dot(u["tinv"], bd(u["power"]))
    for u in us:
        w = _dot(bd(u["tinv"]), u["rhs_w"])
        u["w_v"] = [w[CHUNK * h:CHUNK * (h + 1), :C_DV] for h in range(nh)]
        u["k_cd"] = [w[CHUNK * h:CHUNK * (h + 1), C_DV:].astype(CDT) for h in range(nh)]

    state = {(b, d, h): s_ref[b, d, h] for b in range(bsz) for d in range(2) for h in range(nh)}
    for t in range(n_chunks):
        act = [(b, d, t if d == 0 else n_chunks - 1 - t) for b in range(bsz) for d in range(2)]
        v_new, q_state = {}, {}
        for key in act:
            u = units[key]
            for h in range(nh):
                x = _dot(jnp.concatenate([u["k_cd"][h], u["qd"][:, C_DK * h:C_DK * (h + 1)]], axis=0),
                         state[key[0], key[1], h])
                v_new[key, h] = u["w_v"][h] - x[:CHUNK]
                q_state[key, h] = x[CHUNK:]
        for key in act:
            u = units[key]
            for h in range(nh):
                sk = (key[0], key[1], h)
                state[sk] = (state[sk] * u["eglast"][:, C_DV * h:C_DV * (h + 1)]
                             + _dot(u["k_decT"][h], v_new[key, h]))
        for key in act:
            b, d, c = key
            av = _dot(bd(units[key]["aqk"]), jnp.concatenate([v_new[key, h] for h in range(nh)], axis=0))
            for h in range(nh):
                io[d][3][b, CHUNK * c:CHUNK * (c + 1), C_DV * h:C_DV * (h + 1)] = (
                    q_state[key, h] + av[CHUNK * h:CHUNK * (h + 1)])
    for (b, d, h), val in state.items():
        s_ref[b, d, h] = val

    @pl.when(step == pl.num_programs(0) - 1)
    def _():
        sN_ref[...] = s_ref[...]


def _gdn_scan(feat, bg, bgT, state0, *, n_chunks):
    bsz, s, _ = feat.shape
    tb = n_chunks * CHUNK
    ns = s // tb
    fwd = lambda i: (0, i, 0)
    rev = lambda i: (0, ns - 1 - i, 0)
    fwdT = lambda i: (0, 0, i)
    revT = lambda i: (0, 0, ns - 1 - i)
    st = lambda i: (0, 0, 0, 0, 0)
    st_shape = (bsz, 2, C_HEADS, C_DK, C_DV)
    return pl.pallas_call(
        functools.partial(_gdn_scan_kernel, n_chunks=n_chunks),
        grid=(ns,),
        in_specs=[pl.BlockSpec((bsz, tb, 3 * CB), fwd), pl.BlockSpec((bsz, tb, 3 * CB), rev),
                  pl.BlockSpec((bsz, tb, LANES), fwd), pl.BlockSpec((bsz, tb, LANES), rev),
                  pl.BlockSpec((bsz, 2 * N_GATE_DIRS, tb), fwdT), pl.BlockSpec((bsz, 2 * N_GATE_DIRS, tb), revT),
                  pl.BlockSpec(st_shape, st)],
        out_specs=[pl.BlockSpec((bsz, tb, C_V), fwd), pl.BlockSpec((bsz, tb, C_V), rev),
                   pl.BlockSpec(st_shape, st)],
        out_shape=[jax.ShapeDtypeStruct((bsz, s, C_V), F32), jax.ShapeDtypeStruct((bsz, s, C_V), F32),
                   jax.ShapeDtypeStruct(st_shape, F32)],
        scratch_shapes=[pltpu.VMEM(st_shape, F32)],
        compiler_params=_cparams("arbitrary"),
        name="gdn_scan",
    )(feat, feat, bg, bg, bgT, bgT, state0)


def _merge_kernel(x_ref, oaT_ref, ob_ref, of_ref, or_ref, z_ref, g0_ref, g1_ref, g2_ref, onorm_ref,
                  wb_ref, wo_ref, gt_ref, out_ref, *, row_mul, row_off):
    b = pl.program_id(0)
    oc = of_ref[0] + or_ref[0]
    parts = []
    for h in range(C_HEADS):
        oh = oc[:, C_DV * h:C_DV * (h + 1)]
        parts.append(oh * lax.rsqrt(jnp.mean(oh * oh, axis=-1, keepdims=True) + EPS) * onorm_ref[...])
    yc = jnp.concatenate(parts, axis=-1) * _silu(z_ref[0].astype(F32))
    m = (_sigmoid(g0_ref[0].astype(F32)) * _dot_tn(oaT_ref[0], wb_ref[0])
         + _sigmoid(g1_ref[0].astype(F32)) * _dot(ob_ref[0], wb_ref[1])
         + _sigmoid(g2_ref[0].astype(F32)) * _dot(yc, wb_ref[2]))
    gt = gt_ref[pl.ds(b * row_mul + row_off, 1), :]
    out_ref[0] = x_ref[0] + gt * _dot(m, wo_ref[...])


def _merge(x, oaT, ob, of, orr, proj, onorm, wb, wo, mod_l, *, row_mul, row_off, tm):
    bsz, s, d = x.shape
    tok = lambda blk: (lambda b, i: (b, i, blk))
    return pl.pallas_call(
        functools.partial(_merge_kernel, row_mul=row_mul, row_off=row_off),
        grid=(bsz, s // tm),
        in_specs=[pl.BlockSpec((1, tm, d), tok(0)),
                  pl.BlockSpec((1, A_Q, tm), lambda b, i: (b, 0, i)),
                  pl.BlockSpec((1, tm, B_W), tok(0)),
                  pl.BlockSpec((1, tm, C_V), tok(0)),
                  pl.BlockSpec((1, tm, C_V), tok(0)),
                  pl.BlockSpec((1, tm, CB), tok(P_Z)),
                  pl.BlockSpec((1, tm, d), tok(0)),
                  pl.BlockSpec((1, tm, d), tok(1)),
                  pl.BlockSpec((1, tm, d), tok(2)),
                  pl.BlockSpec((1, C_DV), lambda b, i: (0, 0)),
                  pl.BlockSpec(wb.shape, lambda b, i: (0, 0, 0)),
                  pl.BlockSpec(wo.shape, lambda b, i: (0, 0)),
                  pl.BlockSpec((8, d), lambda b, i: (0, 2))],
        out_specs=pl.BlockSpec((1, tm, d), tok(0)),
        out_shape=jax.ShapeDtypeStruct((bsz, s, d), F32),
        compiler_params=_cparams("arbitrary", "arbitrary"),
        name="merge",
    )(x, oaT, ob, of, orr, proj, proj, proj, proj, onorm.reshape(1, C_DV), wb, wo, mod_l)


def _mlp_kernel(x_ref, sh_ref, sc_ref, gt_ref, gain_ref, w1_ref, w2_ref, fin_ref, out_ref, h_ref, acc_ref,
                *, row_mul, row_off, final_norm):
    b = pl.program_id(0)
    f = pl.program_id(2)
    row = b * row_mul + row_off

    @pl.when(f == 0)
    def _():
        x = x_ref[0]
        r = lax.rsqrt(jnp.mean(x * x, axis=-1, keepdims=True) + EPS)
        h = (x * r * gain_ref[...]) * (1.0 + sc_ref[pl.ds(row, 1), :]) + sh_ref[pl.ds(row, 1), :]
        h_ref[...] = h.astype(CDT)
        acc_ref[...] = jnp.zeros_like(acc_ref)

    a = jnp.maximum(jnp.dot(h_ref[...], w1_ref[...], preferred_element_type=F32), 0.0)
    acc_ref[...] += _dot(a * a, w2_ref[...])

    @pl.when(f == pl.num_programs(2) - 1)
    def _():
        y = x_ref[0] + gt_ref[pl.ds(row, 1), :] * acc_ref[...]
        if final_norm:
            y = y * lax.rsqrt(jnp.mean(y * y, axis=-1, keepdims=True) + EPS) * fin_ref[...]
        out_ref[0] = y


def _mlp(x, mod_l, gain, w1, w2, fin, *, row_mul, row_off, tm, tf, final_norm):
    bsz, s, d = x.shape
    dff = w1.shape[1]
    return pl.pallas_call(
        functools.partial(_mlp_kernel, row_mul=row_mul, row_off=row_off, final_norm=final_norm),
        grid=(bsz, s // tm, dff // tf),
        in_specs=[pl.BlockSpec((1, tm, d), lambda b, i, f: (b, i, 0)),
                  pl.BlockSpec((8, d), lambda b, i, f: (0, 3)),
                  pl.BlockSpec((8, d), lambda b, i, f: (0, 4)),
                  pl.BlockSpec((8, d), lambda b, i, f: (0, 5)),
                  pl.BlockSpec((1, d), lambda b, i, f: (0, 0)),
                  pl.BlockSpec((d, tf), lambda b, i, f: (0, f)),
                  pl.BlockSpec((tf, d), lambda b, i, f: (f, 0)),
                  pl.BlockSpec((1, d), lambda b, i, f: (0, 0))],
        out_specs=pl.BlockSpec((1, tm, d), lambda b, i, f: (b, i, 0)),
        out_shape=jax.ShapeDtypeStruct((bsz, s, d), F32),
        scratch_shapes=[pltpu.VMEM((tm, d), CDT), pltpu.VMEM((tm, d), F32)],
        compiler_params=_cparams("arbitrary", "arbitrary", "arbitrary"),
        name="mlp",
    )(x, mod_l, mod_l, mod_l, gain.reshape(1, d), w1, w2, fin.reshape(1, d))


def _reorder_w_in(w_in):
    o = 0
    seg = {}
    for name, width in (("qa", A_Q), ("ka", A_KV), ("va", A_KV), ("qb", B_W), ("kb", B_W), ("vb", B_W),
                        ("qkv", 2 * C_K + C_V), ("z", C_V), ("beta", N_GATE_DIRS), ("a", N_GATE_DIRS),
                        ("gate", None)):
        width = w_in.shape[-1] - o if width is None else width
        seg[name] = w_in[..., o:o + width]
        o += width
    pad = jnp.zeros(w_in.shape[:-1] + (CB - 2 * A_KV - 2 * N_GATE_DIRS,), w_in.dtype)
    cols = [seg["qa"], seg["ka"], seg["va"], seg["beta"], seg["a"], pad, seg["gate"], seg["qb"], seg["kb"],
            seg["vb"], seg["qkv"], seg["z"]]
    return jnp.concatenate(cols, axis=-1).astype(CDT)


def _rope_tables(n_tok, gain_q, gain_k, rotate):
    if rotate:
        t = jnp.arange(n_tok)
        row = (t // GRID_W).astype(F32)
        col = (t % GRID_W).astype(F32)
        axis_dim = HEAD_DIM // 2
        freqs = ROPE_BASE ** (-jnp.arange(0, axis_dim, 2, dtype=F32) / axis_dim)
        ang = jnp.concatenate([row[:, None] * freqs, col[:, None] * freqs], axis=-1)
        cos = jnp.repeat(jnp.cos(ang), 2, axis=-1).T
        sin = jnp.repeat(jnp.sin(ang), 2, axis=-1).T
    else:
        cos = jnp.ones((HEAD_DIM, n_tok), F32)
        sin = jnp.zeros((HEAD_DIM, n_tok), F32)
    even = (jnp.arange(HEAD_DIM) % 2 == 0)[:, None]

    def tables(g, scale):
        g = g.astype(F32) * scale
        return [cos * g[:, None],
                jnp.where(even, -sin * jnp.roll(g, -1)[:, None], 0.0),
                jnp.where(even, 0.0, sin * jnp.roll(g, 1)[:, None])]

    return jnp.stack(tables(gain_q, HEAD_DIM ** -0.5 * math.log2(math.e)) + tables(gain_k, 1.0))


def _decay_params(a_log, dt_bias):
    row = lambda v: jnp.zeros((LANES,), F32).at[N_GATE_DIRS:2 * N_GATE_DIRS].set(v.reshape(-1).astype(F32))
    return jnp.zeros((8, LANES), F32).at[0].set(row(a_log)).at[1].set(row(dt_bias))


def kernel(x, c, ctx, c_ctx, w_mod, b_mod, norm_mix, w_in, q_norm_a, k_norm_a, rpb_b, conv_c, a_log_c, dt_bias_c,
           o_norm_c, w_branch, w_out, norm_ffn, w_ffn1, w_ffn2, norm_final):
    bsz, s, d = x.shape
    lc = ctx.shape[1]
    depth = w_mod.shape[0]
    assert bsz < 8 and s % 1024 == 0 and lc == 256 and d % CB == 0

    tm = 1024
    tk = 512
    cc = jnp.zeros((8, d), F32).at[:bsz].set(c).at[bsz].set(c_ctx)
    mod = _modulation(cc, w_mod, b_mod)
    w_in_r = _reorder_w_in(w_in)
    wb = w_branch.astype(CDT)
    wo = w_out.astype(CDT)
    w1 = w_ffn1.astype(CDT)
    w2 = w_ffn2.astype(CDT)
    lat = dict(row_mul=1, row_off=0)
    con = dict(row_mul=0, row_off=bsz)
    zero_state = jnp.zeros((bsz, 2, C_HEADS, C_DK, C_DV), F32)

    xc = ctx
    for l in range(depth):
        need_ctx = l < depth - 1
        par = _decay_params(a_log_c[l], dt_bias_c[l])
        rope_l = _rope_tables(s, q_norm_a[l], k_norm_a[l], True)
        rope_c = _rope_tables(lc, q_norm_a[l], k_norm_a[l], False)
        bias = _natten_bias(rpb_b[l])
        conv_w = conv_c[l].astype(F32)

        P = _inproj(x, mod[l], norm_mix[l], w_in_r[l:l + 1], rope_l, par, tm=tm, tk=tk, **lat)
        Pc = _inproj(xc, mod[l], norm_mix[l], w_in_r[l:l + 1], rope_c, par, tm=lc, tk=lc, **con)

        o_aT = _gqa(P["qT"], Pc["k"], Pc["vT"], P["k"], P["vT"], q_norm_a[l], k_norm_a[l], tq=512)
        o_b = _natten(P["proj"], Pc["proj"], bias, rows_per_step=8)
        featc = _gdn_features(Pc["proj"], conv_w, tm=lc)
        feat = _gdn_features(P["proj"], conv_w, tm=512)
        oc_f, oc_r, state = _gdn_scan(featc, Pc["bg"], Pc["bgT"], zero_state, n_chunks=2)
        o_f, o_r, _ = _gdn_scan(feat, P["bg"], P["bgT"], state, n_chunks=2)

        last = l == depth - 1
        x = _merge(x, o_aT, o_b, o_f, o_r, P["proj"], o_norm_c[l], wb[l], wo[l], mod[l], tm=tm, **lat)
        x = _mlp(x, mod[l], norm_ffn[l], w1[l], w2[l], norm_final, tm=tm, tf=1024, final_norm=last, **lat)
        if need_ctx:
            oc_aT = _gqa_call(Pc["qT"], Pc["k"], Pc["vT"], None, None, tq=lc, online=True)
            oc_b = _ctx_attn_b(Pc["proj"])
            xc = _merge(xc, oc_aT, oc_b, oc_f, oc_r, Pc["proj"], o_norm_c[l], wb[l], wo[l], mod[l], tm=lc, **con)
            xc = _mlp(xc, mod[l], norm_ffn[l], w1[l], w2[l], norm_final, tm=lc, tf=1024, final_norm=False, **con)
    return x
```

```python
import functools
import math

import jax
import jax.numpy as jnp
from jax import lax
from jax.experimental import pallas as pl
from jax.experimental.pallas import tpu as pltpu

F32 = jnp.float32
CDT = jnp.bfloat16

GRID_W = 64
HEAD_DIM = 64
A_HEADS = 8
A_KV_HEADS = 2
B_HEADS = 8
WIN_R = 8
WIN_C = 16
C_HEADS = 4
C_DK = 128
C_DV = 128
CONV_K = 5
CHUNK = 64
N_BRANCH = 3
ROPE_BASE = 10000.0
EPS = 1e-6

A_Q = A_HEADS * HEAD_DIM
A_KV = A_KV_HEADS * HEAD_DIM
B_W = B_HEADS * HEAD_DIM
C_K = C_HEADS * C_DK
C_V = C_HEADS * C_DV
N_GATE_DIRS = 2 * C_HEADS

LANES = 128
F32_SUBLANES = 8
BF16_SUBLANES = 16
VMEM_LIMIT = 56 * 1024 * 1024

V_ROWS = HEAD_DIM + BF16_SUBLANES
MAX_UNSHIFTED_LOG2_SCORE = 100.0

CB = 512
J_QA, J_KV, J_GATE0, J_QB, J_KB, J_VB, J_GDN0, J_Z = 0, 1, 2, 8, 9, 10, 11, 14
N_JBLK = 15
PROJ_OFF = 2
P_QB, P_KB, P_VB, P_GDN0, P_Z = (J_QB - PROJ_OFF, J_KB - PROJ_OFF, J_VB - PROJ_OFF, J_GDN0 - PROJ_OFF,
                                 J_Z - PROJ_OFF)
N_PBLK = N_JBLK - PROJ_OFF
P_GATE_PAIRS = (0, 1, 2)

LOG2E = math.log2(math.e)
QB_SCALE = HEAD_DIM ** -0.5 * LOG2E


def _cparams(*sem):
    return pltpu.CompilerParams(dimension_semantics=sem, vmem_limit_bytes=VMEM_LIMIT)


def _dot(a, b):
    return jnp.dot(a.astype(CDT), b.astype(CDT), preferred_element_type=F32)


def _dot_nt(a, b):
    return lax.dot_general(a.astype(CDT), b.astype(CDT), (((1,), (1,)), ((), ())), preferred_element_type=F32)


def _dot_tn(a, b):
    return lax.dot_general(a.astype(CDT), b.astype(CDT), (((0,), (0,)), ((), ())), preferred_element_type=F32)


def _sigmoid(x):
    return 1.0 / (1.0 + jnp.exp(-x))


def _silu(x):
    return x * _sigmoid(x)


def _mod_kernel(c_ref, w_ref, b_ref, o_ref):
    o_ref[0] = _dot(_silu(c_ref[...]), w_ref[0]) + b_ref[0]


def _modulation(cc, w_mod, b_mod):
    depth, d, n = w_mod.shape
    tn = 1024
    return pl.pallas_call(
        _mod_kernel,
        grid=(depth, n // tn),
        in_specs=[pl.BlockSpec((8, d), lambda l, j: (0, 0)),
                  pl.BlockSpec((1, d, tn), lambda l, j: (l, 0, j)),
                  pl.BlockSpec((1, 1, tn), lambda l, j: (l, 0, j))],
        out_specs=pl.BlockSpec((1, 8, tn), lambda l, j: (l, 0, j)),
        out_shape=jax.ShapeDtypeStruct((depth, 8, n), F32),
        compiler_params=_cparams("arbitrary", "arbitrary"),
        name="modulation",
    )(cc, w_mod, b_mod.reshape(depth, 1, n))


def _head_norm_rope(xT, c, sa, sb):
    n = xT * lax.rsqrt(jnp.mean(xT * xT, axis=0, keepdims=True) + EPS)
    return n * c + pltpu.roll(n, HEAD_DIM - 1, 0) * sa + pltpu.roll(n, 1, 0) * sb


def _inproj_kernel(x_ref, sh_ref, sc_ref, gain_ref, w_ref, rope_ref, par_ref,
                   qT_ref, k_ref, vT_ref, bg_ref, bgT_ref, proj_ref, h_ref, *, row_mul, row_off):
    b = pl.program_id(0)
    tm = x_ref.shape[1]
    x = x_ref[0]
    r = lax.rsqrt(jnp.mean(x * x, axis=-1, keepdims=True) + EPS)
    row = b * row_mul + row_off
    h_ref[...] = ((x * r * gain_ref[...]) * (1.0 + sc_ref[pl.ds(row, 1), :]) + sh_ref[pl.ds(row, 1), :]).astype(CDT)

    def mm(j):
        return jnp.dot(h_ref[...], w_ref[j], preferred_element_type=F32)

    t = mm(J_QA).T
    zeros = jnp.zeros((HEAD_DIM, tm), F32)
    for h in range(A_HEADS):
        o = _head_norm_rope(t[HEAD_DIM * h:HEAD_DIM * (h + 1)], rope_ref[0], rope_ref[1], rope_ref[2])
        full = jnp.concatenate([o, zeros] if h < A_HEADS // A_KV_HEADS else [zeros, o], axis=0)
        qT_ref[0, h] = full.astype(CDT)

    acc = mm(J_KV)
    t = acc[:, :A_KV].T
    kT = jnp.concatenate(
        [_head_norm_rope(t[HEAD_DIM * g:HEAD_DIM * (g + 1)], rope_ref[3], rope_ref[4], rope_ref[5])
         for g in range(A_KV_HEADS)], axis=0)
    k_ref[0] = kT.T.astype(CDT)
    vT = acc[:, A_KV:2 * A_KV].T
    ones_pad = (lax.broadcasted_iota(jnp.int32, (V_ROWS - HEAD_DIM, tm), 0) == 0).astype(F32)
    vT_ref[0] = jnp.concatenate([piece for g in range(A_KV_HEADS)
                                 for piece in (vT[HEAD_DIM * g:HEAD_DIM * (g + 1)], ones_pad)], axis=0).astype(CDT)
    raw = acc[:, 2 * A_KV:2 * A_KV + LANES]
    lane = lax.broadcasted_iota(jnp.int32, (tm, LANES), 1)
    rowi = lax.broadcasted_iota(jnp.int32, (tm, LANES), 0) % CHUNK
    beta = _sigmoid(raw)
    y = raw + par_ref[1:2, :]
    g = -jnp.exp(par_ref[0:1, :]) * (jnp.maximum(y, 0.0) + jnp.log1p(jnp.exp(-jnp.abs(y))))
    fwd = g
    rev = g
    for s in (1, 2, 4, 8, 16, 32):
        fwd = fwd + jnp.where(rowi >= s, pltpu.roll(fwd, s, 0), 0.0)
        rev = rev + jnp.where(rowi < CHUNK - s, pltpu.roll(rev, tm - s, 0), 0.0)
    is_rev = (lane >= N_GATE_DIRS + C_HEADS) & (lane < 2 * N_GATE_DIRS)
    bg = jnp.where(lane < N_GATE_DIRS, beta, jnp.where(is_rev, rev, fwd))
    bg = jnp.where(lane < 2 * N_GATE_DIRS, bg, 0.0)
    bg_ref[0] = bg
    bgT_ref[0] = bg.T[:2 * N_GATE_DIRS]

    for jj in range(N_PBLK):
        acc = mm(jj + PROJ_OFF)
        proj_ref[0, jj] = (acc * QB_SCALE if jj == P_QB else acc).astype(CDT)


def _inproj(x, mod_l, gain, w_l, rope, par, *, row_mul, row_off, tm):
    bsz, s, d = x.shape
    kern = functools.partial(_inproj_kernel, row_mul=row_mul, row_off=row_off)
    outs = pl.pallas_call(
        kern,
        grid=(bsz, s // tm),
        in_specs=[pl.BlockSpec((1, tm, d), lambda b, i: (b, i, 0)),
                  pl.BlockSpec((8, d), lambda b, i: (0, 0)),
                  pl.BlockSpec((8, d), lambda b, i: (0, 1)),
                  pl.BlockSpec((1, d), lambda b, i: (0, 0)),
                  pl.BlockSpec((N_JBLK, d, CB), lambda b, i: (0, 0, 0), pipeline_mode=pl.Buffered(1)),
                  pl.BlockSpec((6, HEAD_DIM, tm), lambda b, i: (0, 0, i)),
                  pl.BlockSpec((8, LANES), lambda b, i: (0, 0))],
        out_specs=[pl.BlockSpec((1, A_HEADS, 2 * HEAD_DIM, tm), lambda b, i: (b, 0, 0, i)),
                   pl.BlockSpec((1, tm, A_KV), lambda b, i: (b, i, 0)),
                   pl.BlockSpec((1, A_KV_HEADS * V_ROWS, tm), lambda b, i: (b, 0, i)),
                   pl.BlockSpec((1, tm, LANES), lambda b, i: (b, i, 0)),
                   pl.BlockSpec((1, 2 * N_GATE_DIRS, tm), lambda b, i: (b, 0, i)),
                   pl.BlockSpec((1, N_PBLK, tm, CB), lambda b, i: (b, 0, i, 0))],
        out_shape=[jax.ShapeDtypeStruct((bsz, A_HEADS, 2 * HEAD_DIM, s), CDT),
                   jax.ShapeDtypeStruct((bsz, s, A_KV), CDT),
                   jax.ShapeDtypeStruct((bsz, A_KV_HEADS * V_ROWS, s), CDT),
                   jax.ShapeDtypeStruct((bsz, s, LANES), F32),
                   jax.ShapeDtypeStruct((bsz, 2 * N_GATE_DIRS, s), F32),
                   jax.ShapeDtypeStruct((bsz, N_PBLK, s, CB), CDT)],
        scratch_shapes=[pltpu.VMEM((tm, d), CDT)],
        compiler_params=_cparams("arbitrary", "arbitrary"),
        name="inproj",
    )(x, mod_l, mod_l, gain.reshape(1, d), w_l, rope, par)
    return dict(zip(("qT", "k", "vT", "bg", "bgT", "proj"), outs))


def _gqa_kernel(q_ref, k_ref, vT_ref, o_ref, m_ref, acc_ref, s_ref, *, online):
    n_chunks, _, tk = vT_ref.shape[1:]
    group = A_HEADS // A_KV_HEADS
    for g in range(A_KV_HEADS):
        rows = slice(g * V_ROWS, (g + 1) * V_ROWS)
        acc_ref[...] = jnp.zeros(acc_ref.shape, F32)
        if online:
            m_ref[...] = jnp.full(m_ref.shape, -jnp.inf, F32)

        def update(hh, s, vTb):
            if online:
                m_old = m_ref[hh]
                m_new = jnp.maximum(m_old, jnp.max(s, axis=0, keepdims=True))
                m_ref[hh] = m_new
                p = jnp.exp2(s - m_new).astype(CDT)
                acc_ref[hh] = jnp.exp2(m_old - m_new) * acc_ref[hh] + jnp.dot(vTb, p, preferred_element_type=F32)
            else:
                acc_ref[hh] += jnp.dot(vTb, jnp.exp2(s).astype(CDT), preferred_element_type=F32)

        def scores(c, hh):
            start = pl.multiple_of(c * tk, tk)
            return jnp.dot(k_ref[0, pl.ds(start, tk), :], q_ref[0, g * group + hh],
                           preferred_element_type=F32)

        s_ref[...] = scores(0, 0)

        def chunk(c, carry):
            vTb = vT_ref[0, c, rows, :]
            s_next = s_ref[...]
            for hh in range(group):
                s = s_next
                if hh + 1 < group:
                    s_next = scores(c, hh + 1)
                else:
                    s_ref[...] = scores(jnp.minimum(c + 1, n_chunks - 1), 0)
                update(hh, s, vTb)
            return carry

        lax.fori_loop(0, n_chunks, chunk, 0)
        for hh in range(group):
            h = g * group + hh
            a = acc_ref[hh]
            o_ref[0, HEAD_DIM * h:HEAD_DIM * (h + 1), :] = (a[:HEAD_DIM] / a[HEAD_DIM:HEAD_DIM + 1]).astype(o_ref.dtype)


def _gqa_call(qT, k, vT, *, tq, online):
    bsz, _, _, sq = qT.shape
    sk = k.shape[1]
    n_chunks, vr, tk = vT.shape[1:]
    group = A_HEADS // A_KV_HEADS
    return pl.pallas_call(
        functools.partial(_gqa_kernel, online=online),
        grid=(bsz, sq // tq),
        in_specs=[pl.BlockSpec((1, A_HEADS, 2 * HEAD_DIM, tq), lambda b, i: (b, 0, 0, i)),
                  pl.BlockSpec((1, sk, A_KV), lambda b, i: (b, 0, 0)),
                  pl.BlockSpec((1, n_chunks, vr, tk), lambda b, i: (b, 0, 0, 0))],
        out_specs=pl.BlockSpec((1, A_Q, tq), lambda b, i: (b, 0, i)),
        out_shape=jax.ShapeDtypeStruct((bsz, A_Q, sq), CDT),
        scratch_shapes=[pltpu.VMEM((group, 1, tq), F32), pltpu.VMEM((group, V_ROWS, tq), F32),
                        pltpu.VMEM((tk, tq), F32)],
        compiler_params=_cparams("arbitrary", "arbitrary"),
        name="gqa_online" if online else "gqa",
    )(qT, k, vT)


def _key_chunks(vT, tk):
    bsz, vr, sk = vT.shape
    return vT.reshape(bsz, vr, sk // tk, tk).transpose(0, 2, 1, 3)


def _gqa(qT, k, vT, gain_q, gain_k, *, tq):
    bound = (HEAD_DIM ** 0.5 * LOG2E) * jnp.max(jnp.abs(gain_q)) * jnp.max(jnp.abs(gain_k))
    return lax.cond(bound < MAX_UNSHIFTED_LOG2_SCORE,
                    functools.partial(_gqa_call, tq=tq, online=False),
                    functools.partial(_gqa_call, tq=tq, online=True),
                    qT, k, vT)


def _pair_queries(qp):
    lane = lax.broadcasted_iota(jnp.int32, qp.shape, 1)
    zero = jnp.zeros_like(qp)
    return jnp.concatenate([jnp.where(lane < HEAD_DIM, qp, zero), jnp.where(lane >= HEAD_DIM, qp, zero)], axis=0)


def _pair_finish(pv, l):
    n = pv.shape[0] // 2
    pv = pv / l
    lane = lax.broadcasted_iota(jnp.int32, (n, 2 * HEAD_DIM), 1)
    return jnp.where(lane < HEAD_DIM, pv[:n], pv[n:])


def _natten_kernel(q_ref, k_ref, v_ref, kc_ref, vc_ref, bias_ref, o_ref, *, rows_per_step, n_rows):
    i = pl.program_id(1)
    win = WIN_R * GRID_W
    n_pairs = B_HEADS // 2

    def window(rr):
        r = i * rows_per_step + rr
        r0 = jnp.clip(r - WIN_R // 2, 0, n_rows - WIN_R)
        return pl.multiple_of(r0 * GRID_W, GRID_W), r0 - r + (WIN_R - 1)

    def scores(rr, p):
        start, cfg = window(rr)
        cols = slice(2 * HEAD_DIM * p, 2 * HEAD_DIM * (p + 1))
        qs = _pair_queries(q_ref[0, 0, GRID_W * rr:GRID_W * (rr + 1), cols])
        s_loc = _dot_nt(qs, k_ref[0, 0, pl.ds(start, win), cols]) + bias_ref[cfg, p]
        return s_loc, _dot_nt(qs, kc_ref[0, 0, :, cols])

    def finish(rr, p, s_loc, s_ctx):
        start, _ = window(rr)
        cols = slice(2 * HEAD_DIM * p, 2 * HEAD_DIM * (p + 1))
        m = jnp.maximum(jnp.max(s_loc, axis=-1, keepdims=True), jnp.max(s_ctx, axis=-1, keepdims=True))
        e_loc = jnp.exp2(s_loc - m)
        e_ctx = jnp.exp2(s_ctx - m)
        l = jnp.sum(e_loc, axis=-1, keepdims=True) + jnp.sum(e_ctx, axis=-1, keepdims=True)
        pv = _dot(e_loc, v_ref[0, 0, pl.ds(start, win), cols]) + _dot(e_ctx, vc_ref[0, 0, :, cols])
        return _pair_finish(pv, l)

    units = [(rr, p) for rr in range(rows_per_step) for p in range(n_pairs)]
    nxt = scores(*units[0])
    outs = []
    for idx, (rr, p) in enumerate(units):
        cur = nxt
        if idx + 1 < len(units):
            nxt = scores(*units[idx + 1])
        outs.append(finish(rr, p, *cur))
        if p == n_pairs - 1:
            o_ref[0, GRID_W * rr:GRID_W * (rr + 1), :] = jnp.concatenate(outs, axis=-1).astype(o_ref.dtype)
            outs = []


def _natten(proj, cproj, bias, *, rows_per_step):
    bsz, _, s, _ = proj.shape
    lc = cproj.shape[2]
    n_rows = s // GRID_W
    tq = rows_per_step * GRID_W
    whole = lambda blk: (lambda b, i: (b, blk, 0, 0))
    return pl.pallas_call(
        functools.partial(_natten_kernel, rows_per_step=rows_per_step, n_rows=n_rows),
        grid=(bsz, s // tq),
        in_specs=[pl.BlockSpec((1, 1, tq, CB), lambda b, i: (b, P_QB, i, 0)),
                  pl.BlockSpec((1, 1, s, CB), whole(P_KB), pipeline_mode=pl.Buffered(1)),
                  pl.BlockSpec((1, 1, s, CB), whole(P_VB), pipeline_mode=pl.Buffered(1)),
                  pl.BlockSpec((1, 1, lc, CB), whole(P_KB)),
                  pl.BlockSpec((1, 1, lc, CB), whole(P_VB)),
                  pl.BlockSpec(bias.shape, lambda b, i: (0, 0, 0, 0), pipeline_mode=pl.Buffered(1))],
        out_specs=pl.BlockSpec((1, tq, B_W), lambda b, i: (b, i, 0)),
        out_shape=jax.ShapeDtypeStruct((bsz, s, B_W), CDT),
        compiler_params=_cparams("arbitrary", "arbitrary"),
        name="natten",
    )(proj, proj, proj, cproj, cproj, bias)


def _ctx_attn_b_kernel(q_ref, k_ref, v_ref, o_ref):
    outs = []
    for p in range(B_HEADS // 2):
        cols = slice(2 * HEAD_DIM * p, 2 * HEAD_DIM * (p + 1))
        qs = _pair_queries(q_ref[0, 0, :, cols])
        s = _dot_nt(qs, k_ref[0, 0, :, cols])
        e = jnp.exp2(s - jnp.max(s, axis=-1, keepdims=True))
        outs.append(_pair_finish(_dot(e, v_ref[0, 0, :, cols]), jnp.sum(e, axis=-1, keepdims=True)))
    o_ref[0] = jnp.concatenate(outs, axis=-1).astype(o_ref.dtype)


def _ctx_attn_b(cproj):
    bsz, _, lc, _ = cproj.shape
    blk = lambda c: pl.BlockSpec((1, 1, lc, CB), lambda b: (b, c, 0, 0))
    return pl.pallas_call(
        _ctx_attn_b_kernel,
        grid=(bsz,),
        in_specs=[blk(P_QB), blk(P_KB), blk(P_VB)],
        out_specs=pl.BlockSpec((1, lc, B_W), lambda b: (b, 0, 0)),
        out_shape=jax.ShapeDtypeStruct((bsz, lc, B_W), CDT),
        compiler_params=_cparams("arbitrary"),
        name="ctx_attn_b",
    )(cproj, cproj, cproj)


def _natten_bias(rpb):
    h = rpb.shape[0]
    nd = 2 * WIN_C - 1
    lead = GRID_W - WIN_C
    w = jnp.pad(rpb.astype(F32), ((0, 0), (0, 0), (lead, 2 * GRID_W - lead - nd)))
    t = jnp.tile(w, (1, 1, GRID_W))[..., :GRID_W * (2 * GRID_W - 1)]
    toe = t.reshape(h, 2 * WIN_R - 1, GRID_W, 2 * GRID_W - 1)[..., GRID_W - 1:]
    qc = jnp.arange(GRID_W)
    c0 = jnp.clip(qc - WIN_C // 2, 0, GRID_W - WIN_C)
    inwin = (qc[None, :] >= c0[:, None]) & (qc[None, :] < c0[:, None] + WIN_C)
    toe = jnp.where(inwin, toe * LOG2E, -1e30)
    tab = jnp.stack([toe[:, cfg:cfg + WIN_R] for cfg in range(WIN_R)])
    return tab.transpose(0, 1, 3, 2, 4).reshape(WIN_R, h // 2, 2 * GRID_W, WIN_R * GRID_W)


def _gdn_feat_kernel(x_ref, p_ref, n_ref, w_ref, o_ref):
    i = pl.program_id(1)
    c = pl.program_id(2)
    tm = x_ref.shape[2]
    hp = BF16_SUBLANES
    x = x_ref[0, 0].astype(F32)
    prev = jnp.where(i > 0, p_ref[0, 0].astype(F32), 0.0)
    nxt = jnp.where(i < pl.num_programs(1) - 1, n_ref[0, 0].astype(F32), 0.0)
    w = w_ref[...]

    def conv(a):
        n = a.shape[0]
        return (w[0:1] * pltpu.roll(a, 2, 0) + w[1:2] * pltpu.roll(a, 1, 0) + w[2:3] * a
                + w[3:4] * pltpu.roll(a, n - 1, 0) + w[4:5] * pltpu.roll(a, n - 2, 0))

    e = F32_SUBLANES
    top = conv(jnp.concatenate([prev, x[:hp]], axis=0))[hp:hp + e]
    bot = conv(jnp.concatenate([x[tm - hp:], nxt], axis=0))[hp - e:hp]
    y = _silu(jnp.concatenate([top, conv(x)[e:tm - e], bot], axis=0))

    @pl.when(c == 2)
    def _():
        o_ref[0] = y.astype(o_ref.dtype)

    @pl.when(c < 2)
    def _():
        scale = jnp.where(c == 0, C_DK ** -0.5, 1.0)
        parts = []
        for h in range(C_HEADS):
            yh = y[:, C_DK * h:C_DK * (h + 1)]
            parts.append(yh * (lax.rsqrt(jnp.sum(yh * yh, axis=-1, keepdims=True) + EPS) * scale))
        o_ref[0] = jnp.concatenate(parts, axis=-1).astype(o_ref.dtype)


def _gdn_features(proj, conv_w, *, tm):
    bsz, _, s, _ = proj.shape
    hp = BF16_SUBLANES
    nh = s // hp
    return pl.pallas_call(
        _gdn_feat_kernel,
        grid=(bsz, s // tm, 3),
        in_specs=[pl.BlockSpec((1, 1, tm, CB), lambda b, i, c: (b, P_GDN0 + c, i, 0)),
                  pl.BlockSpec((1, 1, hp, CB),
                               lambda b, i, c: (b, P_GDN0 + c, jnp.maximum(i * (tm // hp) - 1, 0), 0)),
                  pl.BlockSpec((1, 1, hp, CB),
                               lambda b, i, c: (b, P_GDN0 + c, jnp.minimum((i + 1) * (tm // hp), nh - 1), 0)),
                  pl.BlockSpec((CONV_K, CB), lambda b, i, c: (0, c))],
        out_specs=pl.BlockSpec((1, tm, CB), lambda b, i, c: (b, i, c)),
        out_shape=jax.ShapeDtypeStruct((bsz, s, 3 * CB), CDT),
        compiler_params=_cparams("arbitrary", "arbitrary", "arbitrary"),
        name="gdn_features",
    )(proj, proj, proj, conv_w)


def _block_diag(x, n, blk):
    lane_blk = lax.broadcasted_iota(jnp.int32, (1, n * blk), 1) // blk
    return jnp.concatenate([jnp.where(lane_blk == h, x, 0.0) for h in range(n)], axis=0)


def _gdn_prepare(d, feat, bgc, bgr):
    nh = C_HEADS
    col0 = d * nh
    last = CHUNK - 1 if d == 0 else 0
    q = feat[:, :C_K].astype(F32)
    k = feat[:, C_K:2 * C_K].astype(F32)
    v = feat[:, 2 * C_K:].astype(F32)

    def wide(c0, width):
        return jnp.concatenate([jnp.broadcast_to(bgc[:, c0 + h:c0 + h + 1], (CHUNK, width)) for h in range(nh)],
                               axis=1)

    beta = wide(col0, C_DK)
    gcol = wide(N_GATE_DIRS + col0, C_DK)
    gcol_c = wide(N_GATE_DIRS + col0, CHUNK)
    grow_c = jnp.concatenate([bgr[N_GATE_DIRS + col0 + h:N_GATE_DIRS + col0 + h + 1, :] for h in range(nh)], axis=1)
    glast = gcol[last:last + 1, :]
    egc = jnp.exp(gcol)
    kb = k * beta
    ii = lax.broadcasted_iota(jnp.int32, (CHUNK, nh * CHUNK), 0)
    jj = lax.broadcasted_iota(jnp.int32, (CHUNK, nh * CHUNK), 1) % CHUNK
    incl = (ii >= jj) if d == 0 else (ii <= jj)
    strict = (ii > jj) if d == 0 else (ii < jj)
    r = _dot_nt(jnp.concatenate([kb, q], axis=0), _block_diag(k, nh, C_DK))
    decay = jnp.exp(jnp.where(incl, gcol_c - grow_c, -jnp.inf))
    nil = jnp.where(strict, -r[:CHUNK] * decay, 0.0)
    rhs_w = jnp.concatenate(
        [jnp.concatenate([(v * beta)[:, C_DV * h:C_DV * (h + 1)], (kb * egc)[:, C_DK * h:C_DK * (h + 1)]], axis=1)
         for h in range(nh)], axis=0).astype(CDT)
    k_dec = k * jnp.exp(glast - gcol)
    return dict(
        tinv=jnp.where(ii == jj, 1.0, 0.0) + nil,
        power=nil,
        aqk=r[CHUNK:] * decay,
        rhs_w=rhs_w,
        qd=(q * egc).astype(CDT),
        k_decT=[k_dec[:, C_DK * h:C_DK * (h + 1)].T.astype(CDT) for h in range(nh)],
        eglast=jnp.exp(glast),
    )


def _gdn_scan_kernel(ff_ref, fr_ref, bgf_ref, bgr_ref, bgTf_ref, bgTr_ref, s0_ref,
                     of_ref, or_ref, sN_ref, s_ref, *, n_chunks):
    step = pl.program_id(0)
    bsz = ff_ref.shape[0]
    nh = C_HEADS
    io = ((ff_ref, bgf_ref, bgTf_ref, of_ref), (fr_ref, bgr_ref, bgTr_ref, or_ref))

    @pl.when(step == 0)
    def _():
        s_ref[...] = s0_ref[...]

    units = {}
    for c in range(n_chunks):
        rows = slice(CHUNK * c, CHUNK * (c + 1))
        for b in range(bsz):
            for d in range(2):
                f_ref, bg_ref, bgT_ref, _ = io[d]
                units[b, d, c] = _gdn_prepare(d, f_ref[b, rows, :], bg_ref[b, rows, :], bgT_ref[b, :, rows])
    us = list(units.values())

    bd = lambda x: _block_diag(x, nh, CHUNK)
    n_sq = int(math.log2(CHUNK)) - 1
    for u in us:
        u["power"] = _dot(u["power"], bd(u["power"]))
    for _ in range(n_sq - 1):
        for u in us:
            r = _dot(jnp.concatenate([u["tinv"], u["power"]], axis=0), bd(u["power"]))
            u["tinv"] = u["tinv"] + r[:CHUNK]
            u["power"] = r[CHUNK:]
    for u in us:
        u["tinv"] = u["tinv"] + _dot(u["tinv"], bd(u["power"]))
    for u in us:
        w = _dot(bd(u["tinv"]), u["rhs_w"])
        u["w_v"] = [w[CHUNK * h:CHUNK * (h + 1), :C_DV] for h in range(nh)]
        u["k_cd"] = [w[CHUNK * h:CHUNK * (h + 1), C_DV:].astype(CDT) for h in range(nh)]

    state = {(b, d, h): s_ref[b, d, h] for b in range(bsz) for d in range(2) for h in range(nh)}
    for t in range(n_chunks):
        act = [(b, d, t if d == 0 else n_chunks - 1 - t) for b in range(bsz) for d in range(2)]
        v_new, q_state = {}, {}
        for key in act:
            u = units[key]
            for h in range(nh):
                x = _dot(jnp.concatenate([u["k_cd"][h], u["qd"][:, C_DK * h:C_DK * (h + 1)]], axis=0),
                         state[key[0], key[1], h])
                v_new[key, h] = u["w_v"][h] - x[:CHUNK]
                q_state[key, h] = x[CHUNK:]
        for key in act:
            u = units[key]
            for h in range(nh):
                sk = (key[0], key[1], h)
                state[sk] = (state[sk] * u["eglast"][:, C_DV * h:C_DV * (h + 1)]
                             + _dot(u["k_decT"][h], v_new[key, h]))
        for key in act:
            b, d, c = key
            av = _dot(bd(units[key]["aqk"]), jnp.concatenate([v_new[key, h] for h in range(nh)], axis=0))
            for h in range(nh):
                io[d][3][b, CHUNK * c:CHUNK * (c + 1), C_DV * h:C_DV * (h + 1)] = (
                    q_state[key, h] + av[CHUNK * h:CHUNK * (h + 1)])
    for (b, d, h), val in state.items():
        s_ref[b, d, h] = val

    @pl.when(step == pl.num_programs(0) - 1)
    def _():
        sN_ref[...] = s_ref[...]


def _gdn_scan(feat, bg, bgT, state0, *, n_chunks):
    bsz, s, _ = feat.shape
    tb = n_chunks * CHUNK
    ns = s // tb
    fwd = lambda i: (0, i, 0)
    rev = lambda i: (0, ns - 1 - i, 0)
    fwdT = lambda i: (0, 0, i)
    revT = lambda i: (0, 0, ns - 1 - i)
    st = lambda i: (0, 0, 0, 0, 0)
    st_shape = (bsz, 2, C_HEADS, C_DK, C_DV)
    return pl.pallas_call(
        functools.partial(_gdn_scan_kernel, n_chunks=n_chunks),
        grid=(ns,),
        in_specs=[pl.BlockSpec((bsz, tb, 3 * CB), fwd), pl.BlockSpec((bsz, tb, 3 * CB), rev),
                  pl.BlockSpec((bsz, tb, LANES), fwd), pl.BlockSpec((bsz, tb, LANES), rev),
                  pl.BlockSpec((bsz, 2 * N_GATE_DIRS, tb), fwdT), pl.BlockSpec((bsz, 2 * N_GATE_DIRS, tb), revT),
                  pl.BlockSpec(st_shape, st)],
        out_specs=[pl.BlockSpec((bsz, tb, C_V), fwd), pl.BlockSpec((bsz, tb, C_V), rev),
                   pl.BlockSpec(st_shape, st)],
        out_shape=[jax.ShapeDtypeStruct((bsz, s, C_V), F32), jax.ShapeDtypeStruct((bsz, s, C_V), F32),
                   jax.ShapeDtypeStruct(st_shape, F32)],
        scratch_shapes=[pltpu.VMEM(st_shape, F32)],
        compiler_params=_cparams("arbitrary"),
        name="gdn_scan",
    )(feat, feat, bg, bg, bgT, bgT, state0)


def _merge_kernel(x_ref, oaT_ref, ob_ref, of_ref, or_ref, z_ref, g0_ref, g1_ref, g2_ref, onorm_ref,
                  wb_ref, wo_ref, gt_ref, out_ref, *, row_mul, row_off):
    b = pl.program_id(0)
    oc = of_ref[0] + or_ref[0]
    parts = []
    for h in range(C_HEADS):
        oh = oc[:, C_DV * h:C_DV * (h + 1)]
        parts.append(oh * lax.rsqrt(jnp.mean(oh * oh, axis=-1, keepdims=True) + EPS) * onorm_ref[...])
    yc = jnp.concatenate(parts, axis=-1) * _silu(z_ref[0, 0].astype(F32))
    gate = lambda g_ref: _sigmoid(jnp.concatenate([g_ref[0, 0], g_ref[0, 1]], axis=-1).astype(F32))
    m = (gate(g0_ref) * _dot_tn(oaT_ref[0], wb_ref[0])
         + gate(g1_ref) * _dot(ob_ref[0], wb_ref[1])
         + gate(g2_ref) * _dot(yc, wb_ref[2]))
    gt = gt_ref[pl.ds(b * row_mul + row_off, 1), :]
    out_ref[0] = x_ref[0] + gt * _dot(m, wo_ref[...])


def _merge(x, oaT, ob, of, orr, proj, onorm, wb, wo, mod_l, *, row_mul, row_off, tm):
    bsz, s, d = x.shape
    assert d == 2 * CB
    tok = lambda blk: (lambda b, i: (b, i, blk))
    gate = lambda n: pl.BlockSpec((1, 2, tm, CB), lambda b, i: (b, n, i, 0))
    return pl.pallas_call(
        functools.partial(_merge_kernel, row_mul=row_mul, row_off=row_off),
        grid=(bsz, s // tm),
        in_specs=[pl.BlockSpec((1, tm, d), tok(0)),
                  pl.BlockSpec((1, A_Q, tm), lambda b, i: (b, 0, i)),
                  pl.BlockSpec((1, tm, B_W), tok(0)),
                  pl.BlockSpec((1, tm, C_V), tok(0)),
                  pl.BlockSpec((1, tm, C_V), tok(0)),
                  pl.BlockSpec((1, 1, tm, CB), lambda b, i: (b, P_Z, i, 0)),
                  gate(P_GATE_PAIRS[0]),
                  gate(P_GATE_PAIRS[1]),
                  gate(P_GATE_PAIRS[2]),
                  pl.BlockSpec((1, C_DV), lambda b, i: (0, 0)),
                  pl.BlockSpec(wb.shape, lambda b, i: (0, 0, 0)),
                  pl.BlockSpec(wo.shape, lambda b, i: (0, 0)),
                  pl.BlockSpec((8, d), lambda b, i: (0, 2))],
        out_specs=pl.BlockSpec((1, tm, d), tok(0)),
        out_shape=jax.ShapeDtypeStruct((bsz, s, d), F32),
        compiler_params=_cparams("arbitrary", "arbitrary"),
        name="merge",
    )(x, oaT, ob, of, orr, proj, proj, proj, proj, onorm.reshape(1, C_DV), wb, wo, mod_l)


def _mlp_kernel(x_ref, sh_ref, sc_ref, gt_ref, gain_ref, w1_ref, w2_ref, fin_ref, out_ref, h_ref, acc_ref,
                *, row_mul, row_off, final_norm):
    b = pl.program_id(0)
    f = pl.program_id(2)
    row = b * row_mul + row_off

    @pl.when(f == 0)
    def _():
        x = x_ref[0]
        r = lax.rsqrt(jnp.mean(x * x, axis=-1, keepdims=True) + EPS)
        h = (x * r * gain_ref[...]) * (1.0 + sc_ref[pl.ds(row, 1), :]) + sh_ref[pl.ds(row, 1), :]
        h_ref[...] = h.astype(CDT)
        acc_ref[...] = jnp.zeros_like(acc_ref)

    a = jnp.maximum(jnp.dot(h_ref[...], w1_ref[...], preferred_element_type=F32), 0.0)
    acc_ref[...] += _dot(a * a, w2_ref[...])

    @pl.when(f == pl.num_programs(2) - 1)
    def _():
        y = x_ref[0] + gt_ref[pl.ds(row, 1), :] * acc_ref[...]
        if final_norm:
            y = y * lax.rsqrt(jnp.mean(y * y, axis=-1, keepdims=True) + EPS) * fin_ref[...]
        out_ref[0] = y


def _mlp(x, mod_l, gain, w1, w2, fin, *, row_mul, row_off, tm, tf, final_norm):
    bsz, s, d = x.shape
    dff = w1.shape[1]
    return pl.pallas_call(
        functools.partial(_mlp_kernel, row_mul=row_mul, row_off=row_off, final_norm=final_norm),
        grid=(bsz, s // tm, dff // tf),
        in_specs=[pl.BlockSpec((1, tm, d), lambda b, i, f: (b, i, 0)),
                  pl.BlockSpec((8, d), lambda b, i, f: (0, 3)),
                  pl.BlockSpec((8, d), lambda b, i, f: (0, 4)),
                  pl.BlockSpec((8, d), lambda b, i, f: (0, 5)),
                  pl.BlockSpec((1, d), lambda b, i, f: (0, 0)),
                  pl.BlockSpec((d, tf), lambda b, i, f: (0, f)),
                  pl.BlockSpec((tf, d), lambda b, i, f: (f, 0)),
                  pl.BlockSpec((1, d), lambda b, i, f: (0, 0))],
        out_specs=pl.BlockSpec((1, tm, d), lambda b, i, f: (b, i, 0)),
        out_shape=jax.ShapeDtypeStruct((bsz, s, d), F32),
        scratch_shapes=[pltpu.VMEM((tm, d), CDT), pltpu.VMEM((tm, d), F32)],
        compiler_params=_cparams("arbitrary", "arbitrary", "arbitrary"),
        name="mlp",
    )(x, mod_l, mod_l, mod_l, gain.reshape(1, d), w1, w2, fin.reshape(1, d))


def _reorder_w_in(w_in):
    o = 0
    seg = {}
    for name, width in (("qa", A_Q), ("ka", A_KV), ("va", A_KV), ("qb", B_W), ("kb", B_W), ("vb", B_W),
                        ("qkv", 2 * C_K + C_V), ("z", C_V), ("beta", N_GATE_DIRS), ("a", N_GATE_DIRS),
                        ("gate", None)):
        width = w_in.shape[-1] - o if width is None else width
        seg[name] = w_in[..., o:o + width]
        o += width
    pad = jnp.zeros(w_in.shape[:-1] + (CB - 2 * A_KV - 2 * N_GATE_DIRS,), w_in.dtype)
    cols = [seg["qa"], seg["ka"], seg["va"], seg["beta"], seg["a"], pad, seg["gate"], seg["qb"], seg["kb"],
            seg["vb"], seg["qkv"], seg["z"]]
    w = jnp.concatenate(cols, axis=-1).astype(CDT)
    return w.reshape(w.shape[:-1] + (N_JBLK, CB)).swapaxes(-2, -3)


def _rope_tables(n_tok, gain_q, gain_k, rotate):
    if rotate:
        t = jnp.arange(n_tok)
        row = (t // GRID_W).astype(F32)
        col = (t % GRID_W).astype(F32)
        axis_dim = HEAD_DIM // 2
        freqs = ROPE_BASE ** (-jnp.arange(0, axis_dim, 2, dtype=F32) / axis_dim)
        ang = jnp.concatenate([row[:, None] * freqs, col[:, None] * freqs], axis=-1)
        cos = jnp.repeat(jnp.cos(ang), 2, axis=-1).T
        sin = jnp.repeat(jnp.sin(ang), 2, axis=-1).T
    else:
        cos = jnp.ones((HEAD_DIM, n_tok), F32)
        sin = jnp.zeros((HEAD_DIM, n_tok), F32)
    even = (jnp.arange(HEAD_DIM) % 2 == 0)[:, None]

    def tables(g, scale):
        g = g.astype(F32) * scale
        return [cos * g[:, None],
                jnp.where(even, -sin * jnp.roll(g, -1)[:, None], 0.0),
                jnp.where(even, 0.0, sin * jnp.roll(g, 1)[:, None])]

    return jnp.stack(tables(gain_q, HEAD_DIM ** -0.5 * math.log2(math.e)) + tables(gain_k, 1.0))


def _decay_params(a_log, dt_bias):
    row = lambda v: jnp.zeros((LANES,), F32).at[N_GATE_DIRS:2 * N_GATE_DIRS].set(v.reshape(-1).astype(F32))
    return jnp.zeros((8, LANES), F32).at[0].set(row(a_log)).at[1].set(row(dt_bias))


def kernel(x, c, ctx, c_ctx, w_mod, b_mod, norm_mix, w_in, q_norm_a, k_norm_a, rpb_b, conv_c, a_log_c, dt_bias_c,
           o_norm_c, w_branch, w_out, norm_ffn, w_ffn1, w_ffn2, norm_final):
    bsz, s, d = x.shape
    lc = ctx.shape[1]
    depth = w_mod.shape[0]
    assert bsz < 8 and s % 1024 == 0 and lc == 256 and d % CB == 0

    tm = 1024
    tk = next(t for t in (768, 640, 512, 384, 256, 128) if (s + lc) % t == 0)
    cc = jnp.zeros((8, d), F32).at[:bsz].set(c).at[bsz].set(c_ctx)
    mod = _modulation(cc, w_mod, b_mod)
    w_in_r = _reorder_w_in(w_in)
    wb = w_branch.astype(CDT)
    wo = w_out.astype(CDT)
    w1 = w_ffn1.astype(CDT)
    w2 = w_ffn2.astype(CDT)
    lat = dict(row_mul=1, row_off=0)
    con = dict(row_mul=0, row_off=bsz)
    zero_state = jnp.zeros((bsz, 2, C_HEADS, C_DK, C_DV), F32)

    xc = ctx
    for l in range(depth):
        need_ctx = l < depth - 1
        par = _decay_params(a_log_c[l], dt_bias_c[l])
        rope_l = _rope_tables(s, q_norm_a[l], k_norm_a[l], True)
        rope_c = _rope_tables(lc, q_norm_a[l], k_norm_a[l], False)
        bias = _natten_bias(rpb_b[l])
        conv_w = conv_c[l].astype(F32)

        P = _inproj(x, mod[l], norm_mix[l], w_in_r[l], rope_l, par, tm=512, **lat)
        Pc = _inproj(xc, mod[l], norm_mix[l], w_in_r[l], rope_c, par, tm=lc, **con)

        k_all = jnp.concatenate([P["k"], Pc["k"]], axis=1)
        vT_all = _key_chunks(jnp.concatenate([P["vT"], Pc["vT"]], axis=2), tk)
        o_aT = _gqa(P["qT"], k_all, vT_all, q_norm_a[l], k_norm_a[l], tq=512)
        o_b = _natten(P["proj"], Pc["proj"], bias, rows_per_step=8)
        featc = _gdn_features(Pc["proj"], conv_w, tm=lc)
        feat = _gdn_features(P["proj"], conv_w, tm=512)
        oc_f, oc_r, state = _gdn_scan(featc, Pc["bg"], Pc["bgT"], zero_state, n_chunks=2)
        o_f, o_r, _ = _gdn_scan(feat, P["bg"], P["bgT"], state, n_chunks=2)

        last = l == depth - 1
        x = _merge(x, o_aT, o_b, o_f, o_r, P["proj"], o_norm_c[l], wb[l], wo[l], mod[l], tm=tm, **lat)
        x = _mlp(x, mod[l], norm_ffn[l], w1[l], w2[l], norm_final, tm=tm, tf=1024, final_norm=last, **lat)
        if need_ctx:
            oc_aT = _gqa_call(Pc["qT"], Pc["k"], _key_chunks(Pc["vT"], lc), tq=lc, online=True)
            oc_b = _ctx_attn_b(Pc["proj"])
            xc = _merge(xc, oc_aT, oc_b, oc_f, oc_r, Pc["proj"], o_norm_c[l], wb[l], wo[l], mod[l], tm=lc, **con)
            xc = _mlp(xc, mod[l], norm_ffn[l], w1[l], w2[l], norm_final, tm=lc, tf=1024, final_norm=False, **con)
    return x
```

```python
import functools
import math

import jax
import jax.numpy as jnp
from jax import lax
from jax.experimental import pallas as pl
from jax.experimental.pallas import tpu as pltpu

F32 = jnp.float32
CDT = jnp.bfloat16

GRID_W = 64
HEAD_DIM = 64
A_HEADS = 8
A_KV_HEADS = 2
B_HEADS = 8
WIN_R = 8
WIN_C = 16
C_HEADS = 4
C_DK = 128
C_DV = 128
CONV_K = 5
CHUNK = 64
N_BRANCH = 3
ROPE_BASE = 10000.0
EPS = 1e-6

A_Q = A_HEADS * HEAD_DIM
A_KV = A_KV_HEADS * HEAD_DIM
B_W = B_HEADS * HEAD_DIM
C_K = C_HEADS * C_DK
C_V = C_HEADS * C_DV
N_GATE_DIRS = 2 * C_HEADS

LANES = 128
MXU_TILE = 256
F32_SUBLANES = 8
BF16_SUBLANES = 16
VMEM_LIMIT = 56 * 1024 * 1024

V_ROWS = HEAD_DIM + BF16_SUBLANES
MAX_UNSHIFTED_LOG2_SCORE = 100.0

CB = 512
J_QA, J_KV, J_GATE0, J_QB, J_KB, J_VB, J_GDN0, J_Z = 0, 1, 2, 8, 9, 10, 11, 14
N_JBLK = 15
PROJ_OFF = 2
P_QB, P_KB, P_VB, P_GDN0, P_Z = (J_QB - PROJ_OFF, J_KB - PROJ_OFF, J_VB - PROJ_OFF, J_GDN0 - PROJ_OFF,
                                 J_Z - PROJ_OFF)
N_PBLK = N_JBLK - PROJ_OFF
P_GATE_PAIRS = (0, 1, 2)

LOG2E = math.log2(math.e)
QB_SCALE = HEAD_DIM ** -0.5 * LOG2E


def _cparams(*sem):
    return pltpu.CompilerParams(dimension_semantics=sem, vmem_limit_bytes=VMEM_LIMIT)


def _dot(a, b):
    return jnp.dot(a.astype(CDT), b.astype(CDT), preferred_element_type=F32)


def _dot_nt(a, b):
    return lax.dot_general(a.astype(CDT), b.astype(CDT), (((1,), (1,)), ((), ())), preferred_element_type=F32)


def _dot_tn(a, b):
    return lax.dot_general(a.astype(CDT), b.astype(CDT), (((0,), (0,)), ((), ())), preferred_element_type=F32)


def _sigmoid(x):
    return 1.0 / (1.0 + jnp.exp(-x))


def _silu(x):
    return x * _sigmoid(x)


def _mod_kernel(c_ref, w_ref, b_ref, o_ref):
    o_ref[0] = _dot(_silu(c_ref[...]), w_ref[0]) + b_ref[0]


def _modulation(cc, w_mod, b_mod):
    depth, d, n = w_mod.shape
    tn = 1024
    return pl.pallas_call(
        _mod_kernel,
        grid=(depth, n // tn),
        in_specs=[pl.BlockSpec((8, d), lambda l, j: (0, 0)),
                  pl.BlockSpec((1, d, tn), lambda l, j: (l, 0, j)),
                  pl.BlockSpec((1, 1, tn), lambda l, j: (l, 0, j))],
        out_specs=pl.BlockSpec((1, 8, tn), lambda l, j: (l, 0, j)),
        out_shape=jax.ShapeDtypeStruct((depth, 8, n), F32),
        compiler_params=_cparams("arbitrary", "arbitrary"),
        name="modulation",
    )(cc, w_mod, b_mod.reshape(depth, 1, n))


def _head_norm_rope(xT, c, sa, sb):
    n = xT * lax.rsqrt(jnp.mean(xT * xT, axis=0, keepdims=True) + EPS)
    return n * c + pltpu.roll(n, HEAD_DIM - 1, 0) * sa + pltpu.roll(n, 1, 0) * sb


def _inproj_kernel(x_ref, sh_ref, sc_ref, gain_ref, w_ref, rope_ref, par_ref,
                   qT_ref, k_ref, vT_ref, bg_ref, bgT_ref, proj_ref, h_ref, *, row_mul, row_off):
    b = pl.program_id(0)
    tm = x_ref.shape[1]
    x = x_ref[0]
    r = lax.rsqrt(jnp.mean(x * x, axis=-1, keepdims=True) + EPS)
    row = b * row_mul + row_off
    h_ref[...] = ((x * r * gain_ref[...]) * (1.0 + sc_ref[pl.ds(row, 1), :]) + sh_ref[pl.ds(row, 1), :]).astype(CDT)

    def mm(j):
        return jnp.dot(h_ref[...], w_ref[:, CB * j:CB * (j + 1)], preferred_element_type=F32)

    t = mm(J_QA).T
    zeros = jnp.zeros((HEAD_DIM, tm), F32)
    for h in range(A_HEADS):
        o = _head_norm_rope(t[HEAD_DIM * h:HEAD_DIM * (h + 1)], rope_ref[0], rope_ref[1], rope_ref[2])
        full = jnp.concatenate([o, zeros] if h < A_HEADS // A_KV_HEADS else [zeros, o], axis=0)
        qT_ref[0, h] = full.astype(CDT)

    acc = mm(J_KV)
    t = acc[:, :A_KV].T
    kT = jnp.concatenate(
        [_head_norm_rope(t[HEAD_DIM * g:HEAD_DIM * (g + 1)], rope_ref[3], rope_ref[4], rope_ref[5])
         for g in range(A_KV_HEADS)], axis=0)
    k_ref[0] = kT.T.astype(CDT)
    vT = acc[:, A_KV:2 * A_KV].T
    ones_pad = (lax.broadcasted_iota(jnp.int32, (V_ROWS - HEAD_DIM, tm), 0) == 0).astype(F32)
    vT_ref[0] = jnp.concatenate([piece for g in range(A_KV_HEADS)
                                 for piece in (vT[HEAD_DIM * g:HEAD_DIM * (g + 1)], ones_pad)], axis=0).astype(CDT)
    raw = acc[:, 2 * A_KV:2 * A_KV + LANES]
    lane = lax.broadcasted_iota(jnp.int32, (tm, LANES), 1)
    rowi = lax.broadcasted_iota(jnp.int32, (tm, LANES), 0) % CHUNK
    beta = _sigmoid(raw)
    y = raw + par_ref[1:2, :]
    g = -jnp.exp(par_ref[0:1, :]) * (jnp.maximum(y, 0.0) + jnp.log1p(jnp.exp(-jnp.abs(y))))
    fwd = g
    rev = g
    for s in (1, 2, 4, 8, 16, 32):
        fwd = fwd + jnp.where(rowi >= s, pltpu.roll(fwd, s, 0), 0.0)
        rev = rev + jnp.where(rowi < CHUNK - s, pltpu.roll(rev, tm - s, 0), 0.0)
    is_rev = (lane >= N_GATE_DIRS + C_HEADS) & (lane < 2 * N_GATE_DIRS)
    bg = jnp.where(lane < N_GATE_DIRS, beta, jnp.where(is_rev, rev, fwd))
    bg = jnp.where(lane < 2 * N_GATE_DIRS, bg, 0.0)
    bg_ref[0] = bg
    bgT_ref[0] = bg.T[:2 * N_GATE_DIRS]

    for jj in range(N_PBLK):
        acc = mm(jj + PROJ_OFF)
        proj_ref[0, jj] = (acc * QB_SCALE if jj == P_QB else acc).astype(CDT)


def _inproj(x, mod_l, gain, w_l, rope, par, *, row_mul, row_off, tm):
    bsz, s, d = x.shape
    kern = functools.partial(_inproj_kernel, row_mul=row_mul, row_off=row_off)
    outs = pl.pallas_call(
        kern,
        grid=(bsz, s // tm),
        in_specs=[pl.BlockSpec((1, tm, d), lambda b, i: (b, i, 0)),
                  pl.BlockSpec((8, d), lambda b, i: (0, 0)),
                  pl.BlockSpec((8, d), lambda b, i: (0, 1)),
                  pl.BlockSpec((1, d), lambda b, i: (0, 0)),
                  pl.BlockSpec((d, N_JBLK * CB), lambda b, i: (0, 0), pipeline_mode=pl.Buffered(1)),
                  pl.BlockSpec((6, HEAD_DIM, tm), lambda b, i: (0, 0, i)),
                  pl.BlockSpec((8, LANES), lambda b, i: (0, 0))],
        out_specs=[pl.BlockSpec((1, A_HEADS, 2 * HEAD_DIM, tm), lambda b, i: (b, 0, 0, i)),
                   pl.BlockSpec((1, tm, A_KV), lambda b, i: (b, i, 0)),
                   pl.BlockSpec((1, A_KV_HEADS * V_ROWS, tm), lambda b, i: (b, 0, i)),
                   pl.BlockSpec((1, tm, LANES), lambda b, i: (b, i, 0)),
                   pl.BlockSpec((1, 2 * N_GATE_DIRS, tm), lambda b, i: (b, 0, i)),
                   pl.BlockSpec((1, N_PBLK, tm, CB), lambda b, i: (b, 0, i, 0))],
        out_shape=[jax.ShapeDtypeStruct((bsz, A_HEADS, 2 * HEAD_DIM, s), CDT),
                   jax.ShapeDtypeStruct((bsz, s, A_KV), CDT),
                   jax.ShapeDtypeStruct((bsz, A_KV_HEADS * V_ROWS, s), CDT),
                   jax.ShapeDtypeStruct((bsz, s, LANES), F32),
                   jax.ShapeDtypeStruct((bsz, 2 * N_GATE_DIRS, s), F32),
                   jax.ShapeDtypeStruct((bsz, N_PBLK, s, CB), CDT)],
        scratch_shapes=[pltpu.VMEM((tm, d), CDT)],
        compiler_params=_cparams("arbitrary", "arbitrary"),
        name="inproj",
    )(x, mod_l, mod_l, gain.reshape(1, d), w_l, rope, par)
    return dict(zip(("qT", "k", "vT", "bg", "bgT", "proj"), outs))


def _gqa_kernel(q_ref, k_ref, vT_ref, o_ref, m_ref, acc_ref, s_ref, *, online):
    n_chunks, _, tk = vT_ref.shape[1:]
    group = A_HEADS // A_KV_HEADS
    for g in range(A_KV_HEADS):
        rows = slice(g * V_ROWS, (g + 1) * V_ROWS)
        acc_ref[...] = jnp.zeros(acc_ref.shape, F32)
        if online:
            m_ref[...] = jnp.full(m_ref.shape, -jnp.inf, F32)

        def scores(c, hh, lo, n):
            start = pl.multiple_of(c * tk, tk)
            return jnp.dot(k_ref[0, pl.ds(start + lo, n), :], q_ref[0, g * group + hh],
                           preferred_element_type=F32)

        s_ref[...] = scores(0, 0, 0, tk)

        def chunk_online(c, carry):
            vTb = vT_ref[0, c, rows, :]
            s_next = s_ref[...]
            for hh in range(group):
                s = s_next
                if hh + 1 < group:
                    s_next = scores(c, hh + 1, 0, tk)
                else:
                    s_ref[...] = scores(jnp.minimum(c + 1, n_chunks - 1), 0, 0, tk)
                m_old = m_ref[hh]
                m_new = jnp.maximum(m_old, jnp.max(s, axis=0, keepdims=True))
                m_ref[hh] = m_new
                p = jnp.exp2(s - m_new).astype(CDT)
                acc_ref[hh] = jnp.exp2(m_old - m_new) * acc_ref[hh] + jnp.dot(vTb, p, preferred_element_type=F32)
            return carry

        def chunk_unshifted(c, carry):
            parts = [(lo, min(MXU_TILE, tk - lo)) for lo in range(0, tk, MXU_TILE)]
            s_cur = [s_ref[lo:lo + n, :] for lo, n in parts]
            for hh in range(group):
                nxt = (c, hh + 1) if hh + 1 < group else (jnp.minimum(c + 1, n_chunks - 1), 0)
                s_nxt = []
                acc = acc_ref[hh]
                for (lo, n), s in zip(parts, s_cur):
                    s_nxt.append(scores(*nxt, lo, n))
                    acc = acc + jnp.dot(vT_ref[0, c, rows, lo:lo + n], jnp.exp2(s).astype(CDT),
                                        preferred_element_type=F32)
                acc_ref[hh] = acc
                s_cur = s_nxt
            for (lo, n), s in zip(parts, s_cur):
                s_ref[lo:lo + n, :] = s
            return carry

        lax.fori_loop(0, n_chunks, chunk_online if online else chunk_unshifted, 0)
        for hh in range(group):
            h = g * group + hh
            a = acc_ref[hh]
            o_ref[0, HEAD_DIM * h:HEAD_DIM * (h + 1), :] = (a[:HEAD_DIM] / a[HEAD_DIM:HEAD_DIM + 1]).astype(o_ref.dtype)


def _gqa_call(qT, k, vT, *, tq, online):
    bsz, _, _, sq = qT.shape
    sk = k.shape[1]
    n_chunks, vr, tk = vT.shape[1:]
    group = A_HEADS // A_KV_HEADS
    return pl.pallas_call(
        functools.partial(_gqa_kernel, online=online),
        grid=(bsz, sq // tq),
        in_specs=[pl.BlockSpec((1, A_HEADS, 2 * HEAD_DIM, tq), lambda b, i: (b, 0, 0, i)),
                  pl.BlockSpec((1, sk, A_KV), lambda b, i: (b, 0, 0)),
                  pl.BlockSpec((1, n_chunks, vr, tk), lambda b, i: (b, 0, 0, 0))],
        out_specs=pl.BlockSpec((1, A_Q, tq), lambda b, i: (b, 0, i)),
        out_shape=jax.ShapeDtypeStruct((bsz, A_Q, sq), CDT),
        scratch_shapes=[pltpu.VMEM((group, 1, tq), F32), pltpu.VMEM((group, V_ROWS, tq), F32),
                        pltpu.VMEM((tk, tq), F32)],
        compiler_params=_cparams("arbitrary", "arbitrary"),
        name="gqa_online" if online else "gqa",
    )(qT, k, vT)


def _key_chunks(vT, tk):
    bsz, vr, sk = vT.shape
    return vT.reshape(bsz, vr, sk // tk, tk).transpose(0, 2, 1, 3)


def _gqa(qT, k, vT, gain_q, gain_k, *, tq):
    bound = (HEAD_DIM ** 0.5 * LOG2E) * jnp.max(jnp.abs(gain_q)) * jnp.max(jnp.abs(gain_k))
    return lax.cond(bound < MAX_UNSHIFTED_LOG2_SCORE,
                    functools.partial(_gqa_call, tq=tq, online=False),
                    functools.partial(_gqa_call, tq=tq, online=True),
                    qT, k, vT)


def _pair_queries(qp):
    lane = lax.broadcasted_iota(jnp.int32, qp.shape, 1)
    zero = jnp.zeros_like(qp)
    return jnp.concatenate([jnp.where(lane < HEAD_DIM, qp, zero), jnp.where(lane >= HEAD_DIM, qp, zero)], axis=0)


def _pair_finish(pv, l):
    n = pv.shape[0] // 2
    pv = pv / l
    lane = lax.broadcasted_iota(jnp.int32, (n, 2 * HEAD_DIM), 1)
    return jnp.where(lane < HEAD_DIM, pv[:n], pv[n:])


def _natten_kernel(q_ref, k_ref, v_ref, kc_ref, vc_ref, bias_ref, o_ref, *, rows_per_step, n_rows):
    i = pl.program_id(1)
    win = WIN_R * GRID_W
    n_pairs = B_HEADS // 2

    def window(rr):
        r = i * rows_per_step + rr
        r0 = jnp.clip(r - WIN_R // 2, 0, n_rows - WIN_R)
        return pl.multiple_of(r0 * GRID_W, GRID_W), r0 - r + (WIN_R - 1)

    def scores(rr, p):
        start, cfg = window(rr)
        cols = slice(2 * HEAD_DIM * p, 2 * HEAD_DIM * (p + 1))
        qs = _pair_queries(q_ref[0, 0, GRID_W * rr:GRID_W * (rr + 1), cols])
        s_loc = _dot_nt(qs, k_ref[0, 0, pl.ds(start, win), cols]) + bias_ref[cfg, p]
        return s_loc, _dot_nt(qs, kc_ref[0, 0, :, cols])

    def finish(rr, p, s_loc, s_ctx):
        start, _ = window(rr)
        cols = slice(2 * HEAD_DIM * p, 2 * HEAD_DIM * (p + 1))
        m = jnp.maximum(jnp.max(s_loc, axis=-1, keepdims=True), jnp.max(s_ctx, axis=-1, keepdims=True))
        e_loc = jnp.exp2(s_loc - m)
        e_ctx = jnp.exp2(s_ctx - m)
        l = jnp.sum(e_loc, axis=-1, keepdims=True) + jnp.sum(e_ctx, axis=-1, keepdims=True)
        pv = _dot(e_loc, v_ref[0, 0, pl.ds(start, win), cols]) + _dot(e_ctx, vc_ref[0, 0, :, cols])
        return _pair_finish(pv, l)

    units = [(rr, p) for rr in range(rows_per_step) for p in range(n_pairs)]
    nxt = scores(*units[0])
    outs = []
    for idx, (rr, p) in enumerate(units):
        cur = nxt
        if idx + 1 < len(units):
            nxt = scores(*units[idx + 1])
        outs.append(finish(rr, p, *cur))
        if p == n_pairs - 1:
            o_ref[0, GRID_W * rr:GRID_W * (rr + 1), :] = jnp.concatenate(outs, axis=-1).astype(o_ref.dtype)
            outs = []


def _natten(proj, cproj, bias, *, rows_per_step):
    bsz, _, s, _ = proj.shape
    lc = cproj.shape[2]
    n_rows = s // GRID_W
    tq = rows_per_step * GRID_W
    whole = lambda blk: (lambda b, i: (b, blk, 0, 0))
    return pl.pallas_call(
        functools.partial(_natten_kernel, rows_per_step=rows_per_step, n_rows=n_rows),
        grid=(bsz, s // tq),
        in_specs=[pl.BlockSpec((1, 1, tq, CB), lambda b, i: (b, P_QB, i, 0)),
                  pl.BlockSpec((1, 1, s, CB), whole(P_KB), pipeline_mode=pl.Buffered(1)),
                  pl.BlockSpec((1, 1, s, CB), whole(P_VB), pipeline_mode=pl.Buffered(1)),
                  pl.BlockSpec((1, 1, lc, CB), whole(P_KB)),
                  pl.BlockSpec((1, 1, lc, CB), whole(P_VB)),
                  pl.BlockSpec(bias.shape, lambda b, i: (0, 0, 0, 0), pipeline_mode=pl.Buffered(1))],
        out_specs=pl.BlockSpec((1, tq, B_W), lambda b, i: (b, i, 0)),
        out_shape=jax.ShapeDtypeStruct((bsz, s, B_W), CDT),
        compiler_params=_cparams("arbitrary", "arbitrary"),
        name="natten",
    )(proj, proj, proj, cproj, cproj, bias)


def _ctx_attn_b_kernel(q_ref, k_ref, v_ref, o_ref):
    outs = []
    for p in range(B_HEADS // 2):
        cols = slice(2 * HEAD_DIM * p, 2 * HEAD_DIM * (p + 1))
        qs = _pair_queries(q_ref[0, 0, :, cols])
        s = _dot_nt(qs, k_ref[0, 0, :, cols])
        e = jnp.exp2(s - jnp.max(s, axis=-1, keepdims=True))
        outs.append(_pair_finish(_dot(e, v_ref[0, 0, :, cols]), jnp.sum(e, axis=-1, keepdims=True)))
    o_ref[0] = jnp.concatenate(outs, axis=-1).astype(o_ref.dtype)


def _ctx_attn_b(cproj):
    bsz, _, lc, _ = cproj.shape
    blk = lambda c: pl.BlockSpec((1, 1, lc, CB), lambda b: (b, c, 0, 0))
    return pl.pallas_call(
        _ctx_attn_b_kernel,
        grid=(bsz,),
        in_specs=[blk(P_QB), blk(P_KB), blk(P_VB)],
        out_specs=pl.BlockSpec((1, lc, B_W), lambda b: (b, 0, 0)),
        out_shape=jax.ShapeDtypeStruct((bsz, lc, B_W), CDT),
        compiler_params=_cparams("arbitrary"),
        name="ctx_attn_b",
    )(cproj, cproj, cproj)


def _natten_bias(rpb):
    h = rpb.shape[0]
    nd = 2 * WIN_C - 1
    lead = GRID_W - WIN_C
    w = jnp.pad(rpb.astype(F32), ((0, 0), (0, 0), (lead, 2 * GRID_W - lead - nd)))
    t = jnp.tile(w, (1, 1, GRID_W))[..., :GRID_W * (2 * GRID_W - 1)]
    toe = t.reshape(h, 2 * WIN_R - 1, GRID_W, 2 * GRID_W - 1)[..., GRID_W - 1:]
    qc = jnp.arange(GRID_W)
    c0 = jnp.clip(qc - WIN_C // 2, 0, GRID_W - WIN_C)
    inwin = (qc[None, :] >= c0[:, None]) & (qc[None, :] < c0[:, None] + WIN_C)
    toe = jnp.where(inwin, toe * LOG2E, -1e30)
    toe = toe.transpose(0, 2, 1, 3)
    tab = jnp.stack([toe[:, :, cfg:cfg + WIN_R] for cfg in range(WIN_R)])
    return tab.reshape(WIN_R, h // 2, 2 * GRID_W, WIN_R * GRID_W)


def _gdn_feat_kernel(x_ref, p_ref, n_ref, w_ref, o_ref, xs_ref):
    i = pl.program_id(1)
    c = pl.program_id(2)
    tm = x_ref.shape[2]
    hp = BF16_SUBLANES
    e = F32_SUBLANES
    xs_ref[0:e] = jnp.where(i > 0, p_ref[0, 0].astype(F32), 0.0)[hp - e:]
    xs_ref[e:e + tm] = x_ref[0, 0].astype(F32)
    xs_ref[e + tm:] = jnp.where(i < pl.num_programs(1) - 1, n_ref[0, 0].astype(F32), 0.0)[:e]
    w = w_ref[...]
    half = CONV_K // 2
    y = w[0:1] * xs_ref[e - half:e - half + tm]
    for j in range(1, CONV_K):
        y = y + w[j:j + 1] * xs_ref[e - half + j:e - half + j + tm]
    y = _silu(y)

    @pl.when(c == 2)
    def _():
        o_ref[0] = y.astype(o_ref.dtype)

    @pl.when(c < 2)
    def _():
        scale = jnp.where(c == 0, C_DK ** -0.5, 1.0)
        parts = []
        for h in range(C_HEADS):
            yh = y[:, C_DK * h:C_DK * (h + 1)]
            parts.append(yh * (lax.rsqrt(jnp.sum(yh * yh, axis=-1, keepdims=True) + EPS) * scale))
        o_ref[0] = jnp.concatenate(parts, axis=-1).astype(o_ref.dtype)


def _gdn_features(proj, conv_w, *, tm):
    bsz, _, s, _ = proj.shape
    hp = BF16_SUBLANES
    nh = s // hp
    return pl.pallas_call(
        _gdn_feat_kernel,
        grid=(bsz, s // tm, 3),
        in_specs=[pl.BlockSpec((1, 1, tm, CB), lambda b, i, c: (b, P_GDN0 + c, i, 0)),
                  pl.BlockSpec((1, 1, hp, CB),
                               lambda b, i, c: (b, P_GDN0 + c, jnp.maximum(i * (tm // hp) - 1, 0), 0)),
                  pl.BlockSpec((1, 1, hp, CB),
                               lambda b, i, c: (b, P_GDN0 + c, jnp.minimum((i + 1) * (tm // hp), nh - 1), 0)),
                  pl.BlockSpec((CONV_K, CB), lambda b, i, c: (0, c))],
        out_specs=pl.BlockSpec((1, tm, CB), lambda b, i, c: (b, i, c)),
        out_shape=jax.ShapeDtypeStruct((bsz, s, 3 * CB), CDT),
        scratch_shapes=[pltpu.VMEM((tm + 2 * F32_SUBLANES, CB), F32)],
        compiler_params=_cparams("arbitrary", "arbitrary", "arbitrary"),
        name="gdn_features",
    )(proj, proj, proj, conv_w)


def _block_diag(x, n, blk):
    lane_blk = lax.broadcasted_iota(jnp.int32, (1, n * blk), 1) // blk
    return jnp.concatenate([jnp.where(lane_blk == h, x, 0.0) for h in range(n)], axis=0)


def _gdn_prepare(d, feat, bgc, bgr):
    nh = C_HEADS
    col0 = d * nh
    last = CHUNK - 1 if d == 0 else 0
    q = feat[:, :C_K].astype(F32)
    k = feat[:, C_K:2 * C_K].astype(F32)
    v = feat[:, 2 * C_K:].astype(F32)

    def wide(c0, width):
        return jnp.concatenate([jnp.broadcast_to(bgc[:, c0 + h:c0 + h + 1], (CHUNK, width)) for h in range(nh)],
                               axis=1)

    beta = wide(col0, C_DK)
    gcol = wide(N_GATE_DIRS + col0, C_DK)
    gcol_c = wide(N_GATE_DIRS + col0, CHUNK)
    grow_c = jnp.concatenate([bgr[N_GATE_DIRS + col0 + h:N_GATE_DIRS + col0 + h + 1, :] for h in range(nh)], axis=1)
    glast = gcol[last:last + 1, :]
    egc = jnp.exp(gcol)
    kb = k * beta
    ii = lax.broadcasted_iota(jnp.int32, (CHUNK, nh * CHUNK), 0)
    jj = lax.broadcasted_iota(jnp.int32, (CHUNK, nh * CHUNK), 1) % CHUNK
    incl = (ii >= jj) if d == 0 else (ii <= jj)
    strict = (ii > jj) if d == 0 else (ii < jj)
    r = _dot_nt(jnp.concatenate([kb, q], axis=0), _block_diag(k, nh, C_DK))
    decay = jnp.exp(jnp.where(incl, gcol_c - grow_c, -jnp.inf))
    nil = jnp.where(strict, -r[:CHUNK] * decay, 0.0)
    rhs_w = jnp.concatenate(
        [jnp.concatenate([(v * beta)[:, C_DV * h:C_DV * (h + 1)], (kb * egc)[:, C_DK * h:C_DK * (h + 1)]], axis=1)
         for h in range(nh)], axis=0).astype(CDT)
    k_dec = k * jnp.exp(glast - gcol)
    return dict(
        tinv=jnp.where(ii == jj, 1.0, 0.0) + nil,
        power=nil,
        aqk=r[CHUNK:] * decay,
        rhs_w=rhs_w,
        qd=(q * egc).astype(CDT),
        k_decT=[k_dec[:, C_DK * h:C_DK * (h + 1)].T.astype(CDT) for h in range(nh)],
        eglast=jnp.exp(glast),
    )


def _gdn_scan_kernel(ff_ref, fr_ref, bgf_ref, bgr_ref, bgTf_ref, bgTr_ref, s0_ref,
                     of_ref, or_ref, sN_ref, s_ref, *, n_chunks):
    step = pl.program_id(0)
    bsz = ff_ref.shape[0]
    nh = C_HEADS
    io = ((ff_ref, bgf_ref, bgTf_ref, of_ref), (fr_ref, bgr_ref, bgTr_ref, or_ref))

    @pl.when(step == 0)
    def _():
        s_ref[...] = s0_ref[...]

    units = {}
    for c in range(n_chunks):
        rows = slice(CHUNK * c, CHUNK * (c + 1))
        for b in range(bsz):
            for d in range(2):
                f_ref, bg_ref, bgT_ref, _ = io[d]
                units[b, d, c] = _gdn_prepare(d, f_ref[b, rows, :], bg_ref[b, rows, :], bgT_ref[b, :, rows])
    us = list(units.values())

    bd = lambda x: _block_diag(x, nh, CHUNK)
    n_sq = int(math.log2(CHUNK)) - 1
    for u in us:
        u["power"] = _dot(u["power"], bd(u["power"]))
    for _ in range(n_sq - 1):
        for u in us:
            r = _dot(jnp.concatenate([u["tinv"], u["power"]], axis=0), bd(u["power"]))
            u["tinv"] = u["tinv"] + r[:CHUNK]
            u["power"] = r[CHUNK:]
    for u in us:
        u["tinv"] = u["tinv"] + _dot(u["tinv"], bd(u["power"]))
    for u in us:
        w = _dot(bd(u["tinv"]), u["rhs_w"])
        u["w_v"] = [w[CHUNK * h:CHUNK * (h + 1), :C_DV] for h in range(nh)]
        u["k_cd"] = [w[CHUNK * h:CHUNK * (h + 1), C_DV:].astype(CDT) for h in range(nh)]

    state = {(b, d, h): s_ref[b, d, h] for b in range(bsz) for d in range(2) for h in range(nh)}
    for t in range(n_chunks):
        act = [(b, d, t if d == 0 else n_chunks - 1 - t) for b in range(bsz) for d in range(2)]
        v_new, q_state = {}, {}
        for key in act:
            u = units[key]
            for h in range(nh):
                x = _dot(jnp.concatenate([u["k_cd"][h], u["qd"][:, C_DK * h:C_DK * (h + 1)]], axis=0),
                         state[key[0], key[1], h])
                v_new[key, h] = u["w_v"][h] - x[:CHUNK]
                q_state[key, h] = x[CHUNK:]
        for key in act:
            u = units[key]
            for h in range(nh):
                sk = (key[0], key[1], h)
                state[sk] = (state[sk] * u["eglast"][:, C_DV * h:C_DV * (h + 1)]
                             + _dot(u["k_decT"][h], v_new[key, h]))
        for key in act:
            b, d, c = key
            av = _dot(bd(units[key]["aqk"]), jnp.concatenate([v_new[key, h] for h in range(nh)], axis=0))
            for h in range(nh):
                io[d][3][b, CHUNK * c:CHUNK * (c + 1), C_DV * h:C_DV * (h + 1)] = (
                    q_state[key, h] + av[CHUNK * h:CHUNK * (h + 1)])
    for (b, d, h), val in state.items():
        s_ref[b, d, h] = val

    @pl.when(step == pl.num_programs(0) - 1)
    def _():
        sN_ref[...] = s_ref[...]


def _gdn_scan(feat, bg, bgT, state0, *, n_chunks):
    bsz, s, _ = feat.shape
    tb = n_chunks * CHUNK
    ns = s // tb
    fwd = lambda i: (0, i, 0)
    rev = lambda i: (0, ns - 1 - i, 0)
    fwdT = lambda i: (0, 0, i)
    revT = lambda i: (0, 0, ns - 1 - i)
    st = lambda i: (0, 0, 0, 0, 0)
    st_shape = (bsz, 2, C_HEADS, C_DK, C_DV)
    return pl.pallas_call(
        functools.partial(_gdn_scan_kernel, n_chunks=n_chunks),
        grid=(ns,),
        in_specs=[pl.BlockSpec((bsz, tb, 3 * CB), fwd), pl.BlockSpec((bsz, tb, 3 * CB), rev),
                  pl.BlockSpec((bsz, tb, LANES), fwd), pl.BlockSpec((bsz, tb, LANES), rev),
                  pl.BlockSpec((bsz, 2 * N_GATE_DIRS, tb), fwdT), pl.BlockSpec((bsz, 2 * N_GATE_DIRS, tb), revT),
                  pl.BlockSpec(st_shape, st)],
        out_specs=[pl.BlockSpec((bsz, tb, C_V), fwd), pl.BlockSpec((bsz, tb, C_V), rev),
                   pl.BlockSpec(st_shape, st)],
        out_shape=[jax.ShapeDtypeStruct((bsz, s, C_V), F32), jax.ShapeDtypeStruct((bsz, s, C_V), F32),
                   jax.ShapeDtypeStruct(st_shape, F32)],
        scratch_shapes=[pltpu.VMEM(st_shape, F32)],
        compiler_params=_cparams("arbitrary"),
        name="gdn_scan",
    )(feat, feat, bg, bg, bgT, bgT, state0)


def _merge_kernel(x_ref, oaT_ref, ob_ref, of_ref, or_ref, z_ref, g0_ref, g1_ref, g2_ref, onorm_ref,
                  wb_ref, wo_ref, gt_ref, out_ref, *, row_mul, row_off):
    b = pl.program_id(0)
    oc = of_ref[0] + or_ref[0]
    parts = []
    for h in range(C_HEADS):
        oh = oc[:, C_DV * h:C_DV * (h + 1)]
        parts.append(oh * lax.rsqrt(jnp.mean(oh * oh, axis=-1, keepdims=True) + EPS) * onorm_ref[...])
    yc = jnp.concatenate(parts, axis=-1) * _silu(z_ref[0, 0].astype(F32))
    gate = lambda g_ref: _sigmoid(jnp.concatenate([g_ref[0, 0], g_ref[0, 1]], axis=-1).astype(F32))
    m = (gate(g0_ref) * _dot_tn(oaT_ref[0], wb_ref[0])
         + gate(g1_ref) * _dot(ob_ref[0], wb_ref[1])
         + gate(g2_ref) * _dot(yc, wb_ref[2]))
    gt = gt_ref[pl.ds(b * row_mul + row_off, 1), :]
    out_ref[0] = x_ref[0] + gt * _dot(m, wo_ref[...])


def _merge(x, oaT, ob, of, orr, proj, onorm, wb, wo, mod_l, *, row_mul, row_off, tm):
    bsz, s, d = x.shape
    assert d == 2 * CB
    tok = lambda blk: (lambda b, i: (b, i, blk))
    gate = lambda n: pl.BlockSpec((1, 2, tm, CB), lambda b, i: (b, n, i, 0))
    return pl.pallas_call(
        functools.partial(_merge_kernel, row_mul=row_mul, row_off=row_off),
        grid=(bsz, s // tm),
        in_specs=[pl.BlockSpec((1, tm, d), tok(0)),
                  pl.BlockSpec((1, A_Q, tm), lambda b, i: (b, 0, i)),
                  pl.BlockSpec((1, tm, B_W), tok(0)),
                  pl.BlockSpec((1, tm, C_V), tok(0)),
                  pl.BlockSpec((1, tm, C_V), tok(0)),
                  pl.BlockSpec((1, 1, tm, CB), lambda b, i: (b, P_Z, i, 0)),
                  gate(P_GATE_PAIRS[0]),
                  gate(P_GATE_PAIRS[1]),
                  gate(P_GATE_PAIRS[2]),
                  pl.BlockSpec((1, C_DV), lambda b, i: (0, 0)),
                  pl.BlockSpec(wb.shape, lambda b, i: (0, 0, 0)),
                  pl.BlockSpec(wo.shape, lambda b, i: (0, 0)),
                  pl.BlockSpec((8, d), lambda b, i: (0, 2))],
        out_specs=pl.BlockSpec((1, tm, d), tok(0)),
        out_shape=jax.ShapeDtypeStruct((bsz, s, d), F32),
        compiler_params=_cparams("arbitrary", "arbitrary"),
        name="merge",
    )(x, oaT, ob, of, orr, proj, proj, proj, proj, onorm.reshape(1, C_DV), wb, wo, mod_l)


def _mlp_kernel(x_ref, sh_ref, sc_ref, gt_ref, gain_ref, w1_ref, w2_ref, fin_ref, out_ref, h_ref, acc_ref,
                *, row_mul, row_off, final_norm):
    b = pl.program_id(0)
    f = pl.program_id(2)
    row = b * row_mul + row_off

    @pl.when(f == 0)
    def _():
        x = x_ref[0]
        r = lax.rsqrt(jnp.mean(x * x, axis=-1, keepdims=True) + EPS)
        h = (x * r * gain_ref[...]) * (1.0 + sc_ref[pl.ds(row, 1), :]) + sh_ref[pl.ds(row, 1), :]
        h_ref[...] = h.astype(CDT)
        acc_ref[...] = jnp.zeros_like(acc_ref)

    a = jnp.maximum(jnp.dot(h_ref[...], w1_ref[...], preferred_element_type=F32), 0.0)
    acc_ref[...] += _dot(a * a, w2_ref[...])

    @pl.when(f == pl.num_programs(2) - 1)
    def _():
        y = x_ref[0] + gt_ref[pl.ds(row, 1), :] * acc_ref[...]
        if final_norm:
            y = y * lax.rsqrt(jnp.mean(y * y, axis=-1, keepdims=True) + EPS) * fin_ref[...]
        out_ref[0] = y


def _mlp(x, mod_l, gain, w1, w2, fin, *, row_mul, row_off, tm, tf, final_norm):
    bsz, s, d = x.shape
    dff = w1.shape[1]
    return pl.pallas_call(
        functools.partial(_mlp_kernel, row_mul=row_mul, row_off=row_off, final_norm=final_norm),
        grid=(bsz, s // tm, dff // tf),
        in_specs=[pl.BlockSpec((1, tm, d), lambda b, i, f: (b, i, 0)),
                  pl.BlockSpec((8, d), lambda b, i, f: (0, 3)),
                  pl.BlockSpec((8, d), lambda b, i, f: (0, 4)),
                  pl.BlockSpec((8, d), lambda b, i, f: (0, 5)),
                  pl.BlockSpec((1, d), lambda b, i, f: (0, 0)),
                  pl.BlockSpec((d, tf), lambda b, i, f: (0, f)),
                  pl.BlockSpec((tf, d), lambda b, i, f: (f, 0)),
                  pl.BlockSpec((1, d), lambda b, i, f: (0, 0))],
        out_specs=pl.BlockSpec((1, tm, d), lambda b, i, f: (b, i, 0)),
        out_shape=jax.ShapeDtypeStruct((bsz, s, d), F32),
        scratch_shapes=[pltpu.VMEM((tm, d), CDT), pltpu.VMEM((tm, d), F32)],
        compiler_params=_cparams("arbitrary", "arbitrary", "arbitrary"),
        name="mlp",
    )(x, mod_l, mod_l, mod_l, gain.reshape(1, d), w1, w2, fin.reshape(1, d))


def _reorder_w_in(w_in):
    o = 0
    seg = {}
    for name, width in (("qa", A_Q), ("ka", A_KV), ("va", A_KV), ("qb", B_W), ("kb", B_W), ("vb", B_W),
                        ("qkv", 2 * C_K + C_V), ("z", C_V), ("beta", N_GATE_DIRS), ("a", N_GATE_DIRS),
                        ("gate", None)):
        width = w_in.shape[-1] - o if width is None else width
        seg[name] = w_in[..., o:o + width]
        o += width
    pad = jnp.zeros(w_in.shape[:-1] + (CB - 2 * A_KV - 2 * N_GATE_DIRS,), w_in.dtype)
    cols = [seg["qa"], seg["ka"], seg["va"], seg["beta"], seg["a"], pad, seg["gate"], seg["qb"], seg["kb"],
            seg["vb"], seg["qkv"], seg["z"]]
    return jnp.concatenate(cols, axis=-1).astype(CDT)


def _rope_tables(n_tok, gain_q, gain_k, rotate):
    if rotate:
        t = jnp.arange(n_tok)
        row = (t // GRID_W).astype(F32)
        col = (t % GRID_W).astype(F32)
        axis_dim = HEAD_DIM // 2
        freqs = ROPE_BASE ** (-jnp.arange(0, axis_dim, 2, dtype=F32) / axis_dim)
        ang = jnp.concatenate([row[:, None] * freqs, col[:, None] * freqs], axis=-1)
        cos = jnp.repeat(jnp.cos(ang), 2, axis=-1).T
        sin = jnp.repeat(jnp.sin(ang), 2, axis=-1).T
    else:
        cos = jnp.ones((HEAD_DIM, n_tok), F32)
        sin = jnp.zeros((HEAD_DIM, n_tok), F32)
    even = (jnp.arange(HEAD_DIM) % 2 == 0)[:, None]

    def tables(g, scale):
        g = g.astype(F32) * scale
        return [cos * g[:, None],
                jnp.where(even, -sin * jnp.roll(g, -1)[:, None], 0.0),
                jnp.where(even, 0.0, sin * jnp.roll(g, 1)[:, None])]

    return jnp.stack(tables(gain_q, HEAD_DIM ** -0.5 * math.log2(math.e)) + tables(gain_k, 1.0))


def _decay_params(a_log, dt_bias):
    row = lambda v: jnp.zeros((LANES,), F32).at[N_GATE_DIRS:2 * N_GATE_DIRS].set(v.reshape(-1).astype(F32))
    return jnp.zeros((8, LANES), F32).at[0].set(row(a_log)).at[1].set(row(dt_bias))


def kernel(x, c, ctx, c_ctx, w_mod, b_mod, norm_mix, w_in, q_norm_a, k_norm_a, rpb_b, conv_c, a_log_c, dt_bias_c,
           o_norm_c, w_branch, w_out, norm_ffn, w_ffn1, w_ffn2, norm_final):
    bsz, s, d = x.shape
    lc = ctx.shape[1]
    depth = w_mod.shape[0]
    assert bsz < 8 and s % 1024 == 0 and lc == 256 and d % CB == 0

    tm = 1024
    tk = next(t for t in (768, 640, 512, 384, 256, 128) if (s + lc) % t == 0)
    cc = jnp.zeros((8, d), F32).at[:bsz].set(c).at[bsz].set(c_ctx)
    mod = _modulation(cc, w_mod, b_mod)
    w_in_r = _reorder_w_in(w_in)
    wb = w_branch.astype(CDT)
    wo = w_out.astype(CDT)
    w1 = w_ffn1.astype(CDT)
    w2 = w_ffn2.astype(CDT)
    lat = dict(row_mul=1, row_off=0)
    con = dict(row_mul=0, row_off=bsz)
    zero_state = jnp.zeros((bsz, 2, C_HEADS, C_DK, C_DV), F32)

    xc = ctx
    for l in range(depth):
        need_ctx = l < depth - 1
        par = _decay_params(a_log_c[l], dt_bias_c[l])
        rope_l = _rope_tables(s, q_norm_a[l], k_norm_a[l], True)
        rope_c = _rope_tables(lc, q_norm_a[l], k_norm_a[l], False)
        bias = _natten_bias(rpb_b[l])
        conv_w = conv_c[l].astype(F32)

        P = _inproj(x, mod[l], norm_mix[l], w_in_r[l], rope_l, par, tm=512, **lat)
        Pc = _inproj(xc, mod[l], norm_mix[l], w_in_r[l], rope_c, par, tm=lc, **con)

        k_all = jnp.concatenate([P["k"], Pc["k"]], axis=1)
        vT_all = _key_chunks(jnp.concatenate([P["vT"], Pc["vT"]], axis=2), tk)
        o_aT = _gqa(P["qT"], k_all, vT_all, q_norm_a[l], k_norm_a[l], tq=512)
        o_b = _natten(P["proj"], Pc["proj"], bias, rows_per_step=8)
        featc = _gdn_features(Pc["proj"], conv_w, tm=lc)
        feat = _gdn_features(P["proj"], conv_w, tm=512)
        oc_f, oc_r, state = _gdn_scan(featc, Pc["bg"], Pc["bgT"], zero_state, n_chunks=2)
        o_f, o_r, _ = _gdn_scan(feat, P["bg"], P["bgT"], state, n_chunks=4)

        last = l == depth - 1
        x = _merge(x, o_aT, o_b, o_f, o_r, P["proj"], o_norm_c[l], wb[l], wo[l], mod[l], tm=tm, **lat)
        x = _mlp(x, mod[l], norm_ffn[l], w1[l], w2[l], norm_final, tm=tm, tf=1024, final_norm=last, **lat)
        if need_ctx:
            oc_aT = _gqa_call(Pc["qT"], Pc["k"], _key_chunks(Pc["vT"], lc), tq=lc, online=True)
            oc_b = _ctx_attn_b(Pc["proj"])
            xc = _merge(xc, oc_aT, oc_b, oc_f, oc_r, Pc["proj"], o_norm_c[l], wb[l], wo[l], mod[l], tm=lc, **con)
            xc = _mlp(xc, mod[l], norm_ffn[l], w1[l], w2[l], norm_final, tm=lc, tf=1024, final_norm=False, **con)
    return x
```

```python
import functools
import math

import jax
import jax.numpy as jnp
from jax import lax
from jax.experimental import pallas as pl
from jax.experimental.pallas import tpu as pltpu

F32 = jnp.float32
CDT = jnp.bfloat16

GRID_W = 64
HEAD_DIM = 64
A_HEADS = 8
A_KV_HEADS = 2
B_HEADS = 8
WIN_R = 8
WIN_C = 16
C_HEADS = 4
C_DK = 128
C_DV = 128
CONV_K = 5
CHUNK = 64
N_BRANCH = 3
ROPE_BASE = 10000.0
EPS = 1e-6

A_Q = A_HEADS * HEAD_DIM
A_KV = A_KV_HEADS * HEAD_DIM
B_W = B_HEADS * HEAD_DIM
C_K = C_HEADS * C_DK
C_V = C_HEADS * C_DV
N_GATE_DIRS = 2 * C_HEADS

LANES = 128
MXU_TILE = 256
F32_SUBLANES = 8
BF16_SUBLANES = 16
VMEM_LIMIT = 56 * 1024 * 1024

MAX_UNSHIFTED_LOG2_SCORE = 100.0

W_QA = 0
W_KA = W_QA + A_Q
W_QB = W_KA + 2 * A_KV
W_KB = W_QB + B_W
W_VB = W_KB + B_W
W_GDN = W_VB + B_W
W_Z = W_GDN + 2 * C_K + C_V
W_MAIN = W_Z + C_V
W_TAIL_GATE = LANES

CB = 512
PROJ_SRC = (tuple((True, W_TAIL_GATE + CB * j) for j in range(6))
            + ((False, W_QB), (False, W_KB), (False, W_VB))
            + tuple((False, W_GDN + CB * j) for j in range(3)) + ((False, W_Z),))
N_PBLK = len(PROJ_SRC)
P_GATE_PAIRS = (0, 1, 2)
P_QB, P_KB, P_VB, P_GDN0, P_Z = 6, 7, 8, 9, 12

LOG2E = math.log2(math.e)
QB_SCALE = HEAD_DIM ** -0.5 * LOG2E


def _cparams(*sem):
    return pltpu.CompilerParams(dimension_semantics=sem, vmem_limit_bytes=VMEM_LIMIT)


def _dot(a, b):
    return jnp.dot(a.astype(CDT), b.astype(CDT), preferred_element_type=F32)


def _dot_nt(a, b):
    return lax.dot_general(a.astype(CDT), b.astype(CDT), (((1,), (1,)), ((), ())), preferred_element_type=F32)


def _dot_tn(a, b):
    return lax.dot_general(a.astype(CDT), b.astype(CDT), (((0,), (0,)), ((), ())), preferred_element_type=F32)


def _sigmoid(x):
    return 1.0 / (1.0 + jnp.exp(-x))


def _silu(x):
    return x * _sigmoid(x)


def _mod_kernel(c_ref, w_ref, b_ref, o_ref):
    o_ref[0] = _dot(_silu(c_ref[...]), w_ref[0]) + b_ref[0]


def _modulation(cc, w_mod, b_mod):
    depth, d, n = w_mod.shape
    tn = 1024
    return pl.pallas_call(
        _mod_kernel,
        grid=(depth, n // tn),
        in_specs=[pl.BlockSpec((8, d), lambda l, j: (0, 0)),
                  pl.BlockSpec((1, d, tn), lambda l, j: (l, 0, j)),
                  pl.BlockSpec((1, 1, tn), lambda l, j: (l, 0, j))],
        out_specs=pl.BlockSpec((1, 8, tn), lambda l, j: (l, 0, j)),
        out_shape=jax.ShapeDtypeStruct((depth, 8, n), F32),
        compiler_params=_cparams("arbitrary", "arbitrary"),
        name="modulation",
    )(cc, w_mod, b_mod.reshape(depth, 1, n))


def _head_norm_rope(xT, c, sa, sb):
    n = xT * lax.rsqrt(jnp.mean(xT * xT, axis=0, keepdims=True) + EPS)
    return n * c + pltpu.roll(n, HEAD_DIM - 1, 0) * sa + pltpu.roll(n, 1, 0) * sb


def _inproj_kernel(x_ref, sh_ref, sc_ref, gain_ref, w_ref, wt_ref, rope_ref, par_ref,
                   qT_ref, k_ref, vT_ref, bg_ref, bgT_ref, proj_ref, h_ref, *, row_mul, row_off):
    b = pl.program_id(0)
    tm = x_ref.shape[1]
    x = x_ref[0]
    r = lax.rsqrt(jnp.mean(x * x, axis=-1, keepdims=True) + EPS)
    row = b * row_mul + row_off
    h_ref[...] = ((x * r * gain_ref[...]) * (1.0 + sc_ref[pl.ds(row, 1), :]) + sh_ref[pl.ds(row, 1), :]).astype(CDT)

    def mm(ref, lo, n):
        return jnp.dot(h_ref[...], ref[:, lo:lo + n], preferred_element_type=F32)

    t = mm(w_ref, W_QA, A_Q).T
    for h in range(A_HEADS):
        o = _head_norm_rope(t[HEAD_DIM * h:HEAD_DIM * (h + 1)], rope_ref[0], rope_ref[1], rope_ref[2])
        qT_ref[0, h] = o.astype(CDT)

    acc = mm(w_ref, W_KA, 2 * A_KV)
    t = acc[:, :A_KV].T
    for g in range(A_KV_HEADS):
        kT = _head_norm_rope(t[HEAD_DIM * g:HEAD_DIM * (g + 1)], rope_ref[3], rope_ref[4], rope_ref[5])
        k_ref[0, g] = kT.T.astype(CDT)
    vT_ref[0] = acc[:, A_KV:].T.astype(CDT)

    raw = mm(wt_ref, 0, LANES)
    lane = lax.broadcasted_iota(jnp.int32, (tm, LANES), 1)
    rowi = lax.broadcasted_iota(jnp.int32, (tm, LANES), 0) % CHUNK
    beta = _sigmoid(raw)
    y = raw + par_ref[1:2, :]
    g = -jnp.exp(par_ref[0:1, :]) * (jnp.maximum(y, 0.0) + jnp.log1p(jnp.exp(-jnp.abs(y))))
    fwd = g
    rev = g
    for s in (1, 2, 4, 8, 16, 32):
        fwd = fwd + jnp.where(rowi >= s, pltpu.roll(fwd, s, 0), 0.0)
        rev = rev + jnp.where(rowi < CHUNK - s, pltpu.roll(rev, tm - s, 0), 0.0)
    is_rev = (lane >= N_GATE_DIRS + C_HEADS) & (lane < 2 * N_GATE_DIRS)
    bg = jnp.where(lane < N_GATE_DIRS, beta, jnp.where(is_rev, rev, fwd))
    bg = jnp.where(lane < 2 * N_GATE_DIRS, bg, 0.0)
    bg_ref[0] = bg
    bgT_ref[0] = bg.T[:2 * N_GATE_DIRS]

    for jj, (in_tail, lo) in enumerate(PROJ_SRC):
        acc = mm(wt_ref if in_tail else w_ref, lo, CB)
        proj_ref[0, jj] = (acc * QB_SCALE if jj == P_QB else acc).astype(CDT)


def _inproj(x, mod_l, gain, w_main, w_tail, rope, par, *, row_mul, row_off, tm):
    bsz, s, d = x.shape
    kern = functools.partial(_inproj_kernel, row_mul=row_mul, row_off=row_off)
    outs = pl.pallas_call(
        kern,
        grid=(bsz, s // tm),
        in_specs=[pl.BlockSpec((1, tm, d), lambda b, i: (b, i, 0)),
                  pl.BlockSpec((8, d), lambda b, i: (0, 0)),
                  pl.BlockSpec((8, d), lambda b, i: (0, 1)),
                  pl.BlockSpec((1, d), lambda b, i: (0, 0)),
                  pl.BlockSpec(w_main.shape, lambda b, i: (0, 0), pipeline_mode=pl.Buffered(1)),
                  pl.BlockSpec(w_tail.shape, lambda b, i: (0, 0), pipeline_mode=pl.Buffered(1)),
                  pl.BlockSpec((6, HEAD_DIM, tm), lambda b, i: (0, 0, i)),
                  pl.BlockSpec((8, LANES), lambda b, i: (0, 0))],
        out_specs=[pl.BlockSpec((1, A_HEADS, HEAD_DIM, tm), lambda b, i: (b, 0, 0, i)),
                   pl.BlockSpec((1, A_KV_HEADS, tm, HEAD_DIM), lambda b, i: (b, 0, i, 0)),
                   pl.BlockSpec((1, A_KV, tm), lambda b, i: (b, 0, i)),
                   pl.BlockSpec((1, tm, LANES), lambda b, i: (b, i, 0)),
                   pl.BlockSpec((1, 2 * N_GATE_DIRS, tm), lambda b, i: (b, 0, i)),
                   pl.BlockSpec((1, N_PBLK, tm, CB), lambda b, i: (b, 0, i, 0))],
        out_shape=[jax.ShapeDtypeStruct((bsz, A_HEADS, HEAD_DIM, s), CDT),
                   jax.ShapeDtypeStruct((bsz, A_KV_HEADS, s, HEAD_DIM), CDT),
                   jax.ShapeDtypeStruct((bsz, A_KV, s), CDT),
                   jax.ShapeDtypeStruct((bsz, s, LANES), F32),
                   jax.ShapeDtypeStruct((bsz, 2 * N_GATE_DIRS, s), F32),
                   jax.ShapeDtypeStruct((bsz, N_PBLK, s, CB), CDT)],
        scratch_shapes=[pltpu.VMEM((tm, d), CDT)],
        compiler_params=_cparams("arbitrary", "arbitrary"),
        name="inproj",
    )(x, mod_l, mod_l, gain.reshape(1, d), w_main, w_tail, rope, par)
    return dict(zip(("qT", "k", "vT", "bg", "bgT", "proj"), outs))


def _gqa_kernel(q_ref, k_ref, vT_ref, o_ref, m_ref, l_ref, acc_ref, s_ref, *, online):
    n_chunks, _, tk = vT_ref.shape[1:]
    group = A_HEADS // A_KV_HEADS
    for g in range(A_KV_HEADS):
        rows = slice(g * HEAD_DIM, (g + 1) * HEAD_DIM)
        acc_ref[...] = jnp.zeros(acc_ref.shape, F32)
        l_ref[...] = jnp.zeros(l_ref.shape, F32)
        if online:
            m_ref[...] = jnp.full(m_ref.shape, -jnp.inf, F32)

        def scores(c, hh, lo, n):
            start = pl.multiple_of(c * tk, tk)
            return jnp.dot(k_ref[0, g, pl.ds(start + lo, n), :], q_ref[0, g * group + hh],
                           preferred_element_type=F32)

        s_ref[...] = scores(0, 0, 0, tk)

        def chunk_online(c, carry):
            vTb = vT_ref[0, c, rows, :]
            s_next = s_ref[...]
            for hh in range(group):
                s = s_next
                if hh + 1 < group:
                    s_next = scores(c, hh + 1, 0, tk)
                else:
                    s_ref[...] = scores(jnp.minimum(c + 1, n_chunks - 1), 0, 0, tk)
                m_old = m_ref[hh]
                m_new = jnp.maximum(m_old, jnp.max(s, axis=0, keepdims=True))
                m_ref[hh] = m_new
                p = jnp.exp2(s - m_new)
                alpha = jnp.exp2(m_old - m_new)
                l_ref[hh] = alpha * l_ref[hh] + jnp.sum(p, axis=0, keepdims=True)
                acc_ref[hh] = alpha * acc_ref[hh] + jnp.dot(vTb, p.astype(CDT), preferred_element_type=F32)
            return carry

        def chunk_unshifted(c, carry):
            parts = [(lo, min(MXU_TILE, tk - lo)) for lo in range(0, tk, MXU_TILE)]
            s_cur = [s_ref[lo:lo + n, :] for lo, n in parts]
            for hh in range(group):
                nxt = (c, hh + 1) if hh + 1 < group else (jnp.minimum(c + 1, n_chunks - 1), 0)
                s_nxt = []
                acc = acc_ref[hh]
                l = l_ref[hh]
                for (lo, n), s in zip(parts, s_cur):
                    s_nxt.append(scores(*nxt, lo, n))
                    p = jnp.exp2(s)
                    l = l + jnp.sum(p, axis=0, keepdims=True)
                    acc = acc + jnp.dot(vT_ref[0, c, rows, lo:lo + n], p.astype(CDT), preferred_element_type=F32)
                acc_ref[hh] = acc
                l_ref[hh] = l
                s_cur = s_nxt
            for (lo, n), s in zip(parts, s_cur):
                s_ref[lo:lo + n, :] = s
            return carry

        lax.fori_loop(0, n_chunks, chunk_online if online else chunk_unshifted, 0)
        for hh in range(group):
            h = g * group + hh
            o_ref[0, HEAD_DIM * h:HEAD_DIM * (h + 1), :] = (acc_ref[hh] / l_ref[hh]).astype(o_ref.dtype)


def _gqa_call(qT, k, vT, *, tq, online):
    bsz, _, _, sq = qT.shape
    sk = k.shape[2]
    n_chunks, vr, tk = vT.shape[1:]
    group = A_HEADS // A_KV_HEADS
    return pl.pallas_call(
        functools.partial(_gqa_kernel, online=online),
        grid=(bsz, sq // tq),
        in_specs=[pl.BlockSpec((1, A_HEADS, HEAD_DIM, tq), lambda b, i: (b, 0, 0, i)),
                  pl.BlockSpec((1, A_KV_HEADS, sk, HEAD_DIM), lambda b, i: (b, 0, 0, 0)),
                  pl.BlockSpec((1, n_chunks, vr, tk), lambda b, i: (b, 0, 0, 0))],
        out_specs=pl.BlockSpec((1, A_Q, tq), lambda b, i: (b, 0, i)),
        out_shape=jax.ShapeDtypeStruct((bsz, A_Q, sq), CDT),
        scratch_shapes=[pltpu.VMEM((group, 1, tq), F32), pltpu.VMEM((group, 1, tq), F32),
                        pltpu.VMEM((group, HEAD_DIM, tq), F32), pltpu.VMEM((tk, tq), F32)],
        compiler_params=_cparams("arbitrary", "arbitrary"),
        name="gqa_online" if online else "gqa",
    )(qT, k, vT)


def _key_chunks(vT, tk):
    bsz, vr, sk = vT.shape
    return vT.reshape(bsz, vr, sk // tk, tk).transpose(0, 2, 1, 3)


def _gqa(qT, k, vT, gain_q, gain_k, *, tq):
    bound = (HEAD_DIM ** 0.5 * LOG2E) * jnp.max(jnp.abs(gain_q)) * jnp.max(jnp.abs(gain_k))
    return lax.cond(bound < MAX_UNSHIFTED_LOG2_SCORE,
                    functools.partial(_gqa_call, tq=tq, online=False),
                    functools.partial(_gqa_call, tq=tq, online=True),
                    qT, k, vT)


def _pair_queries(qp):
    lane = lax.broadcasted_iota(jnp.int32, qp.shape, 1)
    zero = jnp.zeros_like(qp)
    return jnp.concatenate([jnp.where(lane < HEAD_DIM, qp, zero), jnp.where(lane >= HEAD_DIM, qp, zero)], axis=0)


def _pair_finish(pv, l):
    n = pv.shape[0] // 2
    pv = pv / l
    lane = lax.broadcasted_iota(jnp.int32, (n, 2 * HEAD_DIM), 1)
    return jnp.where(lane < HEAD_DIM, pv[:n], pv[n:])


def _natten_kernel(q_ref, k_ref, v_ref, kc_ref, vc_ref, bias_ref, o_ref, *, rows_per_step, n_rows):
    i = pl.program_id(1)
    win = WIN_R * GRID_W
    n_pairs = B_HEADS // 2

    def window(rr):
        r = i * rows_per_step + rr
        r0 = jnp.clip(r - WIN_R // 2, 0, n_rows - WIN_R)
        return pl.multiple_of(r0 * GRID_W, GRID_W), r0 - r + (WIN_R - 1)

    def scores(rr, p):
        start, cfg = window(rr)
        cols = slice(2 * HEAD_DIM * p, 2 * HEAD_DIM * (p + 1))
        qs = _pair_queries(q_ref[0, 0, GRID_W * rr:GRID_W * (rr + 1), cols])
        s_loc = _dot_nt(qs, k_ref[0, 0, pl.ds(start, win), cols]) + bias_ref[cfg, p]
        return s_loc, _dot_nt(qs, kc_ref[0, 0, :, cols])

    def finish(rr, p, s_loc, s_ctx):
        start, _ = window(rr)
        cols = slice(2 * HEAD_DIM * p, 2 * HEAD_DIM * (p + 1))
        m = jnp.maximum(jnp.max(s_loc, axis=-1, keepdims=True), jnp.max(s_ctx, axis=-1, keepdims=True))
        e_loc = jnp.exp2(s_loc - m)
        e_ctx = jnp.exp2(s_ctx - m)
        l = jnp.sum(e_loc, axis=-1, keepdims=True) + jnp.sum(e_ctx, axis=-1, keepdims=True)
        pv = _dot(e_loc, v_ref[0, 0, pl.ds(start, win), cols]) + _dot(e_ctx, vc_ref[0, 0, :, cols])
        return _pair_finish(pv, l)

    units = [(rr, p) for rr in range(rows_per_step) for p in range(n_pairs)]
    nxt = scores(*units[0])
    outs = []
    for idx, (rr, p) in enumerate(units):
        cur = nxt
        if idx + 1 < len(units):
            nxt = scores(*units[idx + 1])
        outs.append(finish(rr, p, *cur))
        if p == n_pairs - 1:
            o_ref[0, GRID_W * rr:GRID_W * (rr + 1), :] = jnp.concatenate(outs, axis=-1).astype(o_ref.dtype)
            outs = []


def _natten(proj, cproj, bias, *, rows_per_step):
    bsz, _, s, _ = proj.shape
    lc = cproj.shape[2]
    n_rows = s // GRID_W
    tq = rows_per_step * GRID_W
    whole = lambda blk: (lambda b, i: (b, blk, 0, 0))
    return pl.pallas_call(
        functools.partial(_natten_kernel, rows_per_step=rows_per_step, n_rows=n_rows),
        grid=(bsz, s // tq),
        in_specs=[pl.BlockSpec((1, 1, tq, CB), lambda b, i: (b, P_QB, i, 0)),
                  pl.BlockSpec((1, 1, s, CB), whole(P_KB), pipeline_mode=pl.Buffered(1)),
                  pl.BlockSpec((1, 1, s, CB), whole(P_VB), pipeline_mode=pl.Buffered(1)),
                  pl.BlockSpec((1, 1, lc, CB), whole(P_KB)),
                  pl.BlockSpec((1, 1, lc, CB), whole(P_VB)),
                  pl.BlockSpec(bias.shape, lambda b, i: (0, 0, 0, 0), pipeline_mode=pl.Buffered(1))],
        out_specs=pl.BlockSpec((1, tq, B_W), lambda b, i: (b, i, 0)),
        out_shape=jax.ShapeDtypeStruct((bsz, s, B_W), CDT),
        compiler_params=_cparams("arbitrary", "arbitrary"),
        name="natten",
    )(proj, proj, proj, cproj, cproj, bias)


def _ctx_attn_b_kernel(q_ref, k_ref, v_ref, o_ref):
    outs = []
    for p in range(B_HEADS // 2):
        cols = slice(2 * HEAD_DIM * p, 2 * HEAD_DIM * (p + 1))
        qs = _pair_queries(q_ref[0, 0, :, cols])
        s = _dot_nt(qs, k_ref[0, 0, :, cols])
        e = jnp.exp2(s - jnp.max(s, axis=-1, keepdims=True))
        outs.append(_pair_finish(_dot(e, v_ref[0, 0, :, cols]), jnp.sum(e, axis=-1, keepdims=True)))
    o_ref[0] = jnp.concatenate(outs, axis=-1).astype(o_ref.dtype)


def _ctx_attn_b(cproj):
    bsz, _, lc, _ = cproj.shape
    blk = lambda c: pl.BlockSpec((1, 1, lc, CB), lambda b: (b, c, 0, 0))
    return pl.pallas_call(
        _ctx_attn_b_kernel,
        grid=(bsz,),
        in_specs=[blk(P_QB), blk(P_KB), blk(P_VB)],
        out_specs=pl.BlockSpec((1, lc, B_W), lambda b: (b, 0, 0)),
        out_shape=jax.ShapeDtypeStruct((bsz, lc, B_W), CDT),
        compiler_params=_cparams("arbitrary"),
        name="ctx_attn_b",
    )(cproj, cproj, cproj)


def _natten_bias(rpb):
    h = rpb.shape[0]
    nd = 2 * WIN_C - 1
    lead = GRID_W - WIN_C
    w = jnp.pad(rpb.astype(F32), ((0, 0), (0, 0), (lead, 2 * GRID_W - lead - nd)))
    t = jnp.tile(w, (1, 1, GRID_W))[..., :GRID_W * (2 * GRID_W - 1)]
    toe = t.reshape(h, 2 * WIN_R - 1, GRID_W, 2 * GRID_W - 1)[..., GRID_W - 1:]
    qc = jnp.arange(GRID_W)
    c0 = jnp.clip(qc - WIN_C // 2, 0, GRID_W - WIN_C)
    inwin = (qc[None, :] >= c0[:, None]) & (qc[None, :] < c0[:, None] + WIN_C)
    toe = jnp.where(inwin, toe * LOG2E, -1e30)
    toe = toe.transpose(0, 2, 1, 3)
    tab = jnp.stack([toe[:, :, cfg:cfg + WIN_R] for cfg in range(WIN_R)])
    return tab.reshape(WIN_R, h // 2, 2 * GRID_W, WIN_R * GRID_W)


def _gdn_feat_kernel(x_ref, p_ref, n_ref, w_ref, o_ref, xs_ref):
    i = pl.program_id(1)
    c = pl.program_id(2)
    tm = x_ref.shape[2]
    hp = BF16_SUBLANES
    e = F32_SUBLANES
    xs_ref[0:e] = jnp.where(i > 0, p_ref[0, 0].astype(F32), 0.0)[hp - e:]
    xs_ref[e:e + tm] = x_ref[0, 0].astype(F32)
    xs_ref[e + tm:] = jnp.where(i < pl.num_programs(1) - 1, n_ref[0, 0].astype(F32), 0.0)[:e]
    w = w_ref[...]
    half = CONV_K // 2
    y = w[0:1] * xs_ref[e - half:e - half + tm]
    for j in range(1, CONV_K):
        y = y + w[j:j + 1] * xs_ref[e - half + j:e - half + j + tm]
    y = _silu(y)

    @pl.when(c == 2)
    def _():
        o_ref[0] = y.astype(o_ref.dtype)

    @pl.when(c < 2)
    def _():
        scale = jnp.where(c == 0, C_DK ** -0.5, 1.0)
        parts = []
        for h in range(C_HEADS):
            yh = y[:, C_DK * h:C_DK * (h + 1)]
            parts.append(yh * (lax.rsqrt(jnp.sum(yh * yh, axis=-1, keepdims=True) + EPS) * scale))
        o_ref[0] = jnp.concatenate(parts, axis=-1).astype(o_ref.dtype)


def _gdn_features(proj, conv_w, *, tm):
    bsz, _, s, _ = proj.shape
    hp = BF16_SUBLANES
    nh = s // hp
    return pl.pallas_call(
        _gdn_feat_kernel,
        grid=(bsz, s // tm, 3),
        in_specs=[pl.BlockSpec((1, 1, tm, CB), lambda b, i, c: (b, P_GDN0 + c, i, 0)),
                  pl.BlockSpec((1, 1, hp, CB),
                               lambda b, i, c: (b, P_GDN0 + c, jnp.maximum(i * (tm // hp) - 1, 0), 0)),
                  pl.BlockSpec((1, 1, hp, CB),
                               lambda b, i, c: (b, P_GDN0 + c, jnp.minimum((i + 1) * (tm // hp), nh - 1), 0)),
                  pl.BlockSpec((CONV_K, CB), lambda b, i, c: (0, c))],
        out_specs=pl.BlockSpec((1, tm, CB), lambda b, i, c: (b, i, c)),
        out_shape=jax.ShapeDtypeStruct((bsz, s, 3 * CB), CDT),
        scratch_shapes=[pltpu.VMEM((tm + 2 * F32_SUBLANES, CB), F32)],
        compiler_params=_cparams("arbitrary", "arbitrary", "arbitrary"),
        name="gdn_features",
    )(proj, proj, proj, conv_w)


def _block_diag(x, n, blk):
    lane_blk = lax.broadcasted_iota(jnp.int32, (1, n * blk), 1) // blk
    return jnp.concatenate([jnp.where(lane_blk == h, x, 0.0) for h in range(n)], axis=0)


def _gdn_prepare(d, feat, bgc, bgr):
    nh = C_HEADS
    col0 = d * nh
    last = CHUNK - 1 if d == 0 else 0
    q = feat[:, :C_K].astype(F32)
    k = feat[:, C_K:2 * C_K].astype(F32)
    v = feat[:, 2 * C_K:].astype(F32)

    def wide(c0, width):
        return jnp.concatenate([jnp.broadcast_to(bgc[:, c0 + h:c0 + h + 1], (CHUNK, width)) for h in range(nh)],
                               axis=1)

    beta = wide(col0, C_DK)
    gcol = wide(N_GATE_DIRS + col0, C_DK)
    gcol_c = wide(N_GATE_DIRS + col0, CHUNK)
    grow_c = jnp.concatenate([bgr[N_GATE_DIRS + col0 + h:N_GATE_DIRS + col0 + h + 1, :] for h in range(nh)], axis=1)
    glast = gcol[last:last + 1, :]
    egc = jnp.exp(gcol)
    kb = k * beta
    ii = lax.broadcasted_iota(jnp.int32, (CHUNK, nh * CHUNK), 0)
    jj = lax.broadcasted_iota(jnp.int32, (CHUNK, nh * CHUNK), 1) % CHUNK
    incl = (ii >= jj) if d == 0 else (ii <= jj)
    strict = (ii > jj) if d == 0 else (ii < jj)
    r = _dot_nt(jnp.concatenate([kb, q], axis=0), _block_diag(k, nh, C_DK))
    decay = jnp.exp(jnp.where(incl, gcol_c - grow_c, -jnp.inf))
    nil = jnp.where(strict, -r[:CHUNK] * decay, 0.0)
    rhs_w = jnp.concatenate(
        [jnp.concatenate([(v * beta)[:, C_DV * h:C_DV * (h + 1)], (kb * egc)[:, C_DK * h:C_DK * (h + 1)]], axis=1)
         for h in range(nh)], axis=0).astype(CDT)
    k_dec = k * jnp.exp(glast - gcol)
    return dict(
        tinv=jnp.where(ii == jj, 1.0, 0.0) + nil,
        power=nil,
        aqk=r[CHUNK:] * decay,
        rhs_w=rhs_w,
        qd=(q * egc).astype(CDT),
        k_decT=[k_dec[:, C_DK * h:C_DK * (h + 1)].T.astype(CDT) for h in range(nh)],
        eglast=jnp.exp(glast),
    )


def _gdn_scan_kernel(ff_ref, fr_ref, bgf_ref, bgr_ref, bgTf_ref, bgTr_ref, s0_ref,
                     of_ref, or_ref, sN_ref, s_ref, *, n_chunks):
    step = pl.program_id(0)
    bsz = ff_ref.shape[0]
    nh = C_HEADS
    io = ((ff_ref, bgf_ref, bgTf_ref, of_ref), (fr_ref, bgr_ref, bgTr_ref, or_ref))

    @pl.when(step == 0)
    def _():
        s_ref[...] = s0_ref[...]

    units = {}
    for c in range(n_chunks):
        rows = slice(CHUNK * c, CHUNK * (c + 1))
        for b in range(bsz):
            for d in range(2):
                f_ref, bg_ref, bgT_ref, _ = io[d]
                units[b, d, c] = _gdn_prepare(d, f_ref[b, rows, :], bg_ref[b, rows, :], bgT_ref[b, :, rows])
    us = list(units.values())

    bd = lambda x: _block_diag(x, nh, CHUNK)
    n_sq = int(math.log2(CHUNK)) - 1
    for u in us:
        u["power"] = _dot(u["power"], bd(u["power"]))
    for _ in range(n_sq - 1):
        for u in us:
            r = _dot(jnp.concatenate([u["tinv"], u["power"]], axis=0), bd(u["power"]))
            u["tinv"] = u["tinv"] + r[:CHUNK]
            u["power"] = r[CHUNK:]
    for u in us:
        u["tinv"] = u["tinv"] + _dot(u["tinv"], bd(u["power"]))
    for u in us:
        w = _dot(bd(u["tinv"]), u["rhs_w"])
        u["w_v"] = [w[CHUNK * h:CHUNK * (h + 1), :C_DV] for h in range(nh)]
        u["k_cd"] = [w[CHUNK * h:CHUNK * (h + 1), C_DV:].astype(CDT) for h in range(nh)]

    state = {(b, d, h): s_ref[b, d, h] for b in range(bsz) for d in range(2) for h in range(nh)}
    for t in range(n_chunks):
        act = [(b, d, t if d == 0 else n_chunks - 1 - t) for b in range(bsz) for d in range(2)]
        v_new, q_state = {}, {}
        for key in act:
            u = units[key]
            for h in range(nh):
                x = _dot(jnp.concatenate([u["k_cd"][h], u["qd"][:, C_DK * h:C_DK * (h + 1)]], axis=0),
                         state[key[0], key[1], h])
                v_new[key, h] = u["w_v"][h] - x[:CHUNK]
                q_state[key, h] = x[CHUNK:]
        for key in act:
            u = units[key]
            for h in range(nh):
                sk = (key[0], key[1], h)
                state[sk] = (state[sk] * u["eglast"][:, C_DV * h:C_DV * (h + 1)]
                             + _dot(u["k_decT"][h], v_new[key, h]))
        for key in act:
            b, d, c = key
            av = _dot(bd(units[key]["aqk"]), jnp.concatenate([v_new[key, h] for h in range(nh)], axis=0))
            for h in range(nh):
                o_ref = io[d][3]
                o_ref[b, CHUNK * c:CHUNK * (c + 1), C_DV * h:C_DV * (h + 1)] = (
                    q_state[key, h] + av[CHUNK * h:CHUNK * (h + 1)]).astype(o_ref.dtype)
    for (b, d, h), val in state.items():
        s_ref[b, d, h] = val

    @pl.when(step == pl.num_programs(0) - 1)
    def _():
        sN_ref[...] = s_ref[...]


def _gdn_scan(feat, bg, bgT, state0, *, n_chunks):
    bsz, s, _ = feat.shape
    tb = n_chunks * CHUNK
    ns = s // tb
    fwd = lambda i: (0, i, 0)
    rev = lambda i: (0, ns - 1 - i, 0)
    fwdT = lambda i: (0, 0, i)
    revT = lambda i: (0, 0, ns - 1 - i)
    st = lambda i: (0, 0, 0, 0, 0)
    st_shape = (bsz, 2, C_HEADS, C_DK, C_DV)
    return pl.pallas_call(
        functools.partial(_gdn_scan_kernel, n_chunks=n_chunks),
        grid=(ns,),
        in_specs=[pl.BlockSpec((bsz, tb, 3 * CB), fwd), pl.BlockSpec((bsz, tb, 3 * CB), rev),
                  pl.BlockSpec((bsz, tb, LANES), fwd), pl.BlockSpec((bsz, tb, LANES), rev),
                  pl.BlockSpec((bsz, 2 * N_GATE_DIRS, tb), fwdT), pl.BlockSpec((bsz, 2 * N_GATE_DIRS, tb), revT),
                  pl.BlockSpec(st_shape, st)],
        out_specs=[pl.BlockSpec((bsz, tb, C_V), fwd), pl.BlockSpec((bsz, tb, C_V), rev),
                   pl.BlockSpec(st_shape, st)],
        out_shape=[jax.ShapeDtypeStruct((bsz, s, C_V), F32), jax.ShapeDtypeStruct((bsz, s, C_V), F32),
                   jax.ShapeDtypeStruct(st_shape, F32)],
        scratch_shapes=[pltpu.VMEM(st_shape, F32)],
        compiler_params=_cparams("arbitrary"),
        name="gdn_scan",
    )(feat, feat, bg, bg, bgT, bgT, state0)


def _merge_kernel(x_ref, oaT_ref, ob_ref, of_ref, or_ref, z_ref, g0_ref, g1_ref, g2_ref, onorm_ref,
                  wb_ref, wo_ref, gt_ref, out_ref, *, row_mul, row_off):
    b = pl.program_id(0)
    oc = of_ref[0].astype(F32) + or_ref[0].astype(F32)
    parts = []
    for h in range(C_HEADS):
        oh = oc[:, C_DV * h:C_DV * (h + 1)]
        parts.append(oh * lax.rsqrt(jnp.mean(oh * oh, axis=-1, keepdims=True) + EPS) * onorm_ref[...])
    yc = jnp.concatenate(parts, axis=-1) * _silu(z_ref[0, 0].astype(F32))
    gate = lambda g_ref: _sigmoid(jnp.concatenate([g_ref[0, 0], g_ref[0, 1]], axis=-1).astype(F32))
    m = (gate(g0_ref) * _dot_tn(oaT_ref[0], wb_ref[0])
         + gate(g1_ref) * _dot(ob_ref[0], wb_ref[1])
         + gate(g2_ref) * _dot(yc, wb_ref[2]))
    gt = gt_ref[pl.ds(b * row_mul + row_off, 1), :]
    out_ref[0] = x_ref[0] + gt * _dot(m, wo_ref[...])


def _merge(x, oaT, ob, of, orr, proj, onorm, wb, wo, mod_l, *, row_mul, row_off, tm):
    bsz, s, d = x.shape
    assert d == 2 * CB
    tok = lambda blk: (lambda b, i: (b, i, blk))
    gate = lambda n: pl.BlockSpec((1, 2, tm, CB), lambda b, i: (b, n, i, 0))
    return pl.pallas_call(
        functools.partial(_merge_kernel, row_mul=row_mul, row_off=row_off),
        grid=(bsz, s // tm),
        in_specs=[pl.BlockSpec((1, tm, d), tok(0)),
                  pl.BlockSpec((1, A_Q, tm), lambda b, i: (b, 0, i)),
                  pl.BlockSpec((1, tm, B_W), tok(0)),
                  pl.BlockSpec((1, tm, C_V), tok(0)),
                  pl.BlockSpec((1, tm, C_V), tok(0)),
                  pl.BlockSpec((1, 1, tm, CB), lambda b, i: (b, P_Z, i, 0)),
                  gate(P_GATE_PAIRS[0]),
                  gate(P_GATE_PAIRS[1]),
                  gate(P_GATE_PAIRS[2]),
                  pl.BlockSpec((1, C_DV), lambda b, i: (0, 0)),
                  pl.BlockSpec(wb.shape, lambda b, i: (0, 0, 0)),
                  pl.BlockSpec(wo.shape, lambda b, i: (0, 0)),
                  pl.BlockSpec((8, d), lambda b, i: (0, 2))],
        out_specs=pl.BlockSpec((1, tm, d), tok(0)),
        out_shape=jax.ShapeDtypeStruct((bsz, s, d), F32),
        compiler_params=_cparams("arbitrary", "arbitrary"),
        name="merge",
    )(x, oaT, ob, of, orr, proj, proj, proj, proj, onorm.reshape(1, C_DV), wb, wo, mod_l)


def _mlp_kernel(x_ref, sh_ref, sc_ref, gt_ref, gain_ref, w1_ref, w2_ref, fin_ref, out_ref, h_ref, acc_ref,
                *, row_mul, row_off, final_norm):
    b = pl.program_id(0)
    f = pl.program_id(2)
    row = b * row_mul + row_off

    @pl.when(f == 0)
    def _():
        x = x_ref[0]
        r = lax.rsqrt(jnp.mean(x * x, axis=-1, keepdims=True) + EPS)
        h = (x * r * gain_ref[...]) * (1.0 + sc_ref[pl.ds(row, 1), :]) + sh_ref[pl.ds(row, 1), :]
        h_ref[...] = h.astype(CDT)
        acc_ref[...] = jnp.zeros_like(acc_ref)

    a = jnp.maximum(jnp.dot(h_ref[...], w1_ref[...], preferred_element_type=F32), 0.0)
    acc_ref[...] += _dot(a * a, w2_ref[...])

    @pl.when(f == pl.num_programs(2) - 1)
    def _():
        y = x_ref[0] + gt_ref[pl.ds(row, 1), :] * acc_ref[...]
        if final_norm:
            y = y * lax.rsqrt(jnp.mean(y * y, axis=-1, keepdims=True) + EPS) * fin_ref[...]
        out_ref[0] = y


def _mlp(x, mod_l, gain, w1, w2, fin, *, row_mul, row_off, tm, tf, final_norm):
    bsz, s, d = x.shape
    dff = w1.shape[1]
    return pl.pallas_call(
        functools.partial(_mlp_kernel, row_mul=row_mul, row_off=row_off, final_norm=final_norm),
        grid=(bsz, s // tm, dff // tf),
        in_specs=[pl.BlockSpec((1, tm, d), lambda b, i, f: (b, i, 0)),
                  pl.BlockSpec((8, d), lambda b, i, f: (0, 3)),
                  pl.BlockSpec((8, d), lambda b, i, f: (0, 4)),
                  pl.BlockSpec((8, d), lambda b, i, f: (0, 5)),
                  pl.BlockSpec((1, d), lambda b, i, f: (0, 0)),
                  pl.BlockSpec((d, tf), lambda b, i, f: (0, f)),
                  pl.BlockSpec((tf, d), lambda b, i, f: (f, 0)),
                  pl.BlockSpec((1, d), lambda b, i, f: (0, 0))],
        out_specs=pl.BlockSpec((1, tm, d), lambda b, i, f: (b, i, 0)),
        out_shape=jax.ShapeDtypeStruct((bsz, s, d), F32),
        scratch_shapes=[pltpu.VMEM((tm, d), CDT), pltpu.VMEM((tm, d), F32)],
        compiler_params=_cparams("arbitrary", "arbitrary", "arbitrary"),
        name="mlp",
    )(x, mod_l, mod_l, mod_l, gain.reshape(1, d), w1, w2, fin.reshape(1, d))


def _split_w_in(w_in):
    small = 2 * N_GATE_DIRS
    pad = jnp.zeros(w_in.shape[:-1] + (W_TAIL_GATE - small,), w_in.dtype)
    tail = jnp.concatenate([w_in[..., W_MAIN:W_MAIN + small], pad, w_in[..., W_MAIN + small:]], axis=-1)
    return w_in[..., :W_MAIN].astype(CDT), tail.astype(CDT)


def _rope_tables(n_tok, gain_q, gain_k, rotate):
    if rotate:
        t = jnp.arange(n_tok)
        row = (t // GRID_W).astype(F32)
        col = (t % GRID_W).astype(F32)
        axis_dim = HEAD_DIM // 2
        freqs = ROPE_BASE ** (-jnp.arange(0, axis_dim, 2, dtype=F32) / axis_dim)
        ang = jnp.concatenate([row[:, None] * freqs, col[:, None] * freqs], axis=-1)
        cos = jnp.repeat(jnp.cos(ang), 2, axis=-1).T
        sin = jnp.repeat(jnp.sin(ang), 2, axis=-1).T
    else:
        cos = jnp.ones((HEAD_DIM, n_tok), F32)
        sin = jnp.zeros((HEAD_DIM, n_tok), F32)
    even = (jnp.arange(HEAD_DIM) % 2 == 0)[:, None]

    def tables(g, scale):
        g = g.astype(F32) * scale
        return [cos * g[:, None],
                jnp.where(even, -sin * jnp.roll(g, -1)[:, None], 0.0),
                jnp.where(even, 0.0, sin * jnp.roll(g, 1)[:, None])]

    return jnp.stack(tables(gain_q, HEAD_DIM ** -0.5 * LOG2E) + tables(gain_k, 1.0))


def _decay_params(a_log, dt_bias):
    row = lambda v: jnp.zeros((LANES,), F32).at[N_GATE_DIRS:2 * N_GATE_DIRS].set(v.reshape(-1).astype(F32))
    return jnp.zeros((8, LANES), F32).at[0].set(row(a_log)).at[1].set(row(dt_bias))


def kernel(x, c, ctx, c_ctx, w_mod, b_mod, norm_mix, w_in, q_norm_a, k_norm_a, rpb_b, conv_c, a_log_c, dt_bias_c,
           o_norm_c, w_branch, w_out, norm_ffn, w_ffn1, w_ffn2, norm_final):
    bsz, s, d = x.shape
    lc = ctx.shape[1]
    depth = w_mod.shape[0]
    assert bsz < 8 and s % 1024 == 0 and lc == 256 and d % CB == 0

    tm = 1024
    tk = next(t for t in (768, 640, 512, 384, 256, 128) if (s + lc) % t == 0)
    cc = jnp.zeros((8, d), F32).at[:bsz].set(c).at[bsz].set(c_ctx)
    mod = _modulation(cc, w_mod, b_mod)
    w_main, w_tail = _split_w_in(w_in)
    wb = w_branch.astype(CDT)
    wo = w_out.astype(CDT)
    w1 = w_ffn1.astype(CDT)
    w2 = w_ffn2.astype(CDT)
    lat = dict(row_mul=1, row_off=0)
    con = dict(row_mul=0, row_off=bsz)
    zero_state = jnp.zeros((bsz, 2, C_HEADS, C_DK, C_DV), F32)

    xc = ctx
    for l in range(depth):
        need_ctx = l < depth - 1
        par = _decay_params(a_log_c[l], dt_bias_c[l])
        rope_l = _rope_tables(s, q_norm_a[l], k_norm_a[l], True)
        rope_c = _rope_tables(lc, q_norm_a[l], k_norm_a[l], False)
        bias = _natten_bias(rpb_b[l])
        conv_w = conv_c[l].astype(F32)

        P = _inproj(x, mod[l], norm_mix[l], w_main[l], w_tail[l], rope_l, par, tm=512, **lat)
        Pc = _inproj(xc, mod[l], norm_mix[l], w_main[l], w_tail[l], rope_c, par, tm=lc, **con)

        k_all = jnp.concatenate([P["k"], Pc["k"]], axis=2)
        vT_all = _key_chunks(jnp.concatenate([P["vT"], Pc["vT"]], axis=2), tk)
        o_aT = _gqa(P["qT"], k_all, vT_all, q_norm_a[l], k_norm_a[l], tq=512)
        o_b = _natten(P["proj"], Pc["proj"], bias, rows_per_step=8)
        featc = _gdn_features(Pc["proj"], conv_w, tm=lc)
        feat = _gdn_features(P["proj"], conv_w, tm=512)
        oc_f, oc_r, state = _gdn_scan(featc, Pc["bg"], Pc["bgT"], zero_state, n_chunks=2)
        o_f, o_r, _ = _gdn_scan(feat, P["bg"], P["bgT"], state, n_chunks=4)

        last = l == depth - 1
        x = _merge(x, o_aT, o_b, o_f, o_r, P["proj"], o_norm_c[l], wb[l], wo[l], mod[l], tm=tm, **lat)
        x = _mlp(x, mod[l], norm_ffn[l], w1[l], w2[l], norm_final, tm=tm, tf=1024, final_norm=last, **lat)
        if need_ctx:
            oc_aT = _gqa_call(Pc["qT"], Pc["k"], _key_chunks(Pc["vT"], lc), tq=lc, online=True)
            oc_b = _ctx_attn_b(Pc["proj"])
            xc = _merge(xc, oc_aT, oc_b, oc_f, oc_r, Pc["proj"], o_norm_c[l], wb[l], wo[l], mod[l], tm=lc, **con)
            xc = _mlp(xc, mod[l], norm_ffn[l], w1[l], w2[l], norm_final, tm=lc, tf=1024, final_norm=False, **con)
    return x
```

```python
import functools
import math

import jax
import jax.numpy as jnp
from jax import lax
from jax.experimental import pallas as pl
from jax.experimental.pallas import tpu as pltpu

F32 = jnp.float32
CDT = jnp.bfloat16

GRID_W = 64
HEAD_DIM = 64
A_HEADS = 8
A_KV_HEADS = 2
B_HEADS = 8
WIN_R = 8
WIN_C = 16
C_HEADS = 4
C_DK = 128
C_DV = 128
CONV_K = 5
CHUNK = 64
N_BRANCH = 3
ROPE_BASE = 10000.0
EPS = 1e-6

A_Q = A_HEADS * HEAD_DIM
A_KV = A_KV_HEADS * HEAD_DIM
B_W = B_HEADS * HEAD_DIM
C_K = C_HEADS * C_DK
C_V = C_HEADS * C_DV
N_GATE_DIRS = 2 * C_HEADS

LANES = 128
MXU_TILE = 256
F32_SUBLANES = 8
BF16_SUBLANES = 16
VMEM_LIMIT = 56 * 1024 * 1024

MAX_UNSHIFTED_LOG2_SCORE = 100.0

W_QA = 0
W_KA = W_QA + A_Q
W_QB = W_KA + 2 * A_KV
W_KB = W_QB + B_W
W_VB = W_KB + B_W
W_GDN = W_VB + B_W
W_Z = W_GDN + 2 * C_K + C_V
W_MAIN = W_Z + C_V
W_TAIL_GATE = LANES

CB = 512
PROJ_SRC = (tuple((True, W_TAIL_GATE + CB * j) for j in range(6))
            + ((False, W_QB), (False, W_KB), (False, W_VB))
            + tuple((False, W_GDN + CB * j) for j in range(3)) + ((False, W_Z),))
N_PBLK = len(PROJ_SRC)
P_GATE_PAIRS = (0, 1, 2)
P_QB, P_KB, P_VB, P_GDN0, P_Z = 6, 7, 8, 9, 12

LOG2E = math.log2(math.e)
QB_SCALE = HEAD_DIM ** -0.5 * LOG2E


def _cparams(*sem):
    return pltpu.CompilerParams(dimension_semantics=sem, vmem_limit_bytes=VMEM_LIMIT)


def _dot(a, b):
    return jnp.dot(a.astype(CDT), b.astype(CDT), preferred_element_type=F32)


def _dot_nt(a, b):
    return lax.dot_general(a.astype(CDT), b.astype(CDT), (((1,), (1,)), ((), ())), preferred_element_type=F32)


def _dot_tn(a, b):
    return lax.dot_general(a.astype(CDT), b.astype(CDT), (((0,), (0,)), ((), ())), preferred_element_type=F32)


def _sigmoid(x):
    return 1.0 / (1.0 + jnp.exp(-x))


def _silu(x):
    return x * _sigmoid(x)


def _mod_kernel(c_ref, w_ref, b_ref, o_ref):
    o_ref[0] = _dot(_silu(c_ref[...]), w_ref[0]) + b_ref[0]


def _modulation(cc, w_mod, b_mod):
    depth, d, n = w_mod.shape
    tn = 1024
    return pl.pallas_call(
        _mod_kernel,
        grid=(depth, n // tn),
        in_specs=[pl.BlockSpec((8, d), lambda l, j: (0, 0)),
                  pl.BlockSpec((1, d, tn), lambda l, j: (l, 0, j)),
                  pl.BlockSpec((1, 1, tn), lambda l, j: (l, 0, j))],
        out_specs=pl.BlockSpec((1, 8, tn), lambda l, j: (l, 0, j)),
        out_shape=jax.ShapeDtypeStruct((depth, 8, n), F32),
        compiler_params=_cparams("arbitrary", "arbitrary"),
        name="modulation",
    )(cc, w_mod, b_mod.reshape(depth, 1, n))


def _head_norm_rope(xT, c, sa, sb):
    n = xT * lax.rsqrt(jnp.mean(xT * xT, axis=0, keepdims=True) + EPS)
    return n * c + pltpu.roll(n, HEAD_DIM - 1, 0) * sa + pltpu.roll(n, 1, 0) * sb


def _inproj_kernel(x_ref, sh_ref, sc_ref, gain_ref, w_ref, wt_ref, rope_ref, par_ref,
                   qT_ref, k_ref, vT_ref, bg_ref, bgT_ref, proj_ref, h_ref, *, row_mul, row_off):
    b = pl.program_id(0)
    tm = x_ref.shape[1]
    x = x_ref[0]
    r = lax.rsqrt(jnp.mean(x * x, axis=-1, keepdims=True) + EPS)
    row = b * row_mul + row_off
    h_ref[...] = ((x * r * gain_ref[...]) * (1.0 + sc_ref[pl.ds(row, 1), :]) + sh_ref[pl.ds(row, 1), :]).astype(CDT)

    def mm(ref, lo, n):
        return jnp.dot(h_ref[...], ref[0, :, lo:lo + n], preferred_element_type=F32)

    t = mm(w_ref, W_QA, A_Q).T
    for h in range(A_HEADS):
        o = _head_norm_rope(t[HEAD_DIM * h:HEAD_DIM * (h + 1)], rope_ref[0], rope_ref[1], rope_ref[2])
        qT_ref[0, h] = o.astype(CDT)

    acc = mm(w_ref, W_KA, 2 * A_KV)
    t = acc[:, :A_KV].T
    for g in range(A_KV_HEADS):
        kT = _head_norm_rope(t[HEAD_DIM * g:HEAD_DIM * (g + 1)], rope_ref[3], rope_ref[4], rope_ref[5])
        k_ref[0, g] = kT.T.astype(CDT)
    vT_ref[0] = acc[:, A_KV:].T.astype(CDT)

    raw = mm(wt_ref, 0, LANES)
    lane = lax.broadcasted_iota(jnp.int32, (tm, LANES), 1)
    rowi = lax.broadcasted_iota(jnp.int32, (tm, LANES), 0) % CHUNK
    beta = _sigmoid(raw)
    y = raw + par_ref[1:2, :]
    g = -jnp.exp(par_ref[0:1, :]) * (jnp.maximum(y, 0.0) + jnp.log1p(jnp.exp(-jnp.abs(y))))
    fwd = g
    rev = g
    for s in (1, 2, 4, 8, 16, 32):
        fwd = fwd + jnp.where(rowi >= s, pltpu.roll(fwd, s, 0), 0.0)
        rev = rev + jnp.where(rowi < CHUNK - s, pltpu.roll(rev, tm - s, 0), 0.0)
    is_rev = (lane >= N_GATE_DIRS + C_HEADS) & (lane < 2 * N_GATE_DIRS)
    bg = jnp.where(lane < N_GATE_DIRS, beta, jnp.where(is_rev, rev, fwd))
    bg = jnp.where(lane < 2 * N_GATE_DIRS, bg, 0.0)
    bg_ref[0] = bg
    bgT_ref[0] = bg.T[:2 * N_GATE_DIRS]

    for jj, (in_tail, lo) in enumerate(PROJ_SRC):
        acc = mm(wt_ref if in_tail else w_ref, lo, CB)
        proj_ref[0, jj] = (acc * QB_SCALE if jj == P_QB else acc).astype(CDT)


def _inproj(x, mod_l, gain, w_all, w_tail, layer, rope, par, *, row_mul, row_off, tm):
    bsz, s, d = x.shape
    kern = functools.partial(_inproj_kernel, row_mul=row_mul, row_off=row_off)
    outs = pl.pallas_call(
        kern,
        grid=(bsz, s // tm),
        in_specs=[pl.BlockSpec((1, tm, d), lambda b, i: (b, i, 0)),
                  pl.BlockSpec((8, d), lambda b, i: (0, 0)),
                  pl.BlockSpec((8, d), lambda b, i: (0, 1)),
                  pl.BlockSpec((1, d), lambda b, i: (0, 0)),
                  pl.BlockSpec((1, d, W_MAIN), lambda b, i: (layer, 0, 0), pipeline_mode=pl.Buffered(1)),
                  pl.BlockSpec((1,) + w_tail.shape[1:], lambda b, i: (layer, 0, 0), pipeline_mode=pl.Buffered(1)),
                  pl.BlockSpec((6, HEAD_DIM, tm), lambda b, i: (0, 0, i)),
                  pl.BlockSpec((8, LANES), lambda b, i: (0, 0))],
        out_specs=[pl.BlockSpec((1, A_HEADS, HEAD_DIM, tm), lambda b, i: (b, 0, 0, i)),
                   pl.BlockSpec((1, A_KV_HEADS, tm, HEAD_DIM), lambda b, i: (b, 0, i, 0)),
                   pl.BlockSpec((1, A_KV, tm), lambda b, i: (b, 0, i)),
                   pl.BlockSpec((1, tm, LANES), lambda b, i: (b, i, 0)),
                   pl.BlockSpec((1, 2 * N_GATE_DIRS, tm), lambda b, i: (b, 0, i)),
                   pl.BlockSpec((1, N_PBLK, tm, CB), lambda b, i: (b, 0, i, 0))],
        out_shape=[jax.ShapeDtypeStruct((bsz, A_HEADS, HEAD_DIM, s), CDT),
                   jax.ShapeDtypeStruct((bsz, A_KV_HEADS, s, HEAD_DIM), CDT),
                   jax.ShapeDtypeStruct((bsz, A_KV, s), CDT),
                   jax.ShapeDtypeStruct((bsz, s, LANES), F32),
                   jax.ShapeDtypeStruct((bsz, 2 * N_GATE_DIRS, s), F32),
                   jax.ShapeDtypeStruct((bsz, N_PBLK, s, CB), CDT)],
        scratch_shapes=[pltpu.VMEM((tm, d), CDT)],
        compiler_params=_cparams("arbitrary", "arbitrary"),
        name="inproj",
    )(x, mod_l, mod_l, gain.reshape(1, d), w_all, w_tail, rope, par)
    return dict(zip(("qT", "k", "vT", "bg", "bgT", "proj"), outs))


def _gqa_kernel(q_ref, k_ref, vT_ref, o_ref, m_ref, l_ref, acc_ref, s_ref, *, online):
    n_chunks, _, tk = vT_ref.shape[1:]
    group = A_HEADS // A_KV_HEADS
    for g in range(A_KV_HEADS):
        rows = slice(g * HEAD_DIM, (g + 1) * HEAD_DIM)
        acc_ref[...] = jnp.zeros(acc_ref.shape, F32)
        l_ref[...] = jnp.zeros(l_ref.shape, F32)
        if online:
            m_ref[...] = jnp.full(m_ref.shape, -jnp.inf, F32)

        def scores(c, hh, lo, n):
            start = pl.multiple_of(c * tk, tk)
            return jnp.dot(k_ref[0, g, pl.ds(start + lo, n), :], q_ref[0, g * group + hh],
                           preferred_element_type=F32)

        s_ref[...] = scores(0, 0, 0, tk)

        def chunk_online(c, carry):
            vTb = vT_ref[0, c, rows, :]
            s_next = s_ref[...]
            for hh in range(group):
                s = s_next
                if hh + 1 < group:
                    s_next = scores(c, hh + 1, 0, tk)
                else:
                    s_ref[...] = scores(jnp.minimum(c + 1, n_chunks - 1), 0, 0, tk)
                m_old = m_ref[hh]
                m_new = jnp.maximum(m_old, jnp.max(s, axis=0, keepdims=True))
                m_ref[hh] = m_new
                p = jnp.exp2(s - m_new)
                alpha = jnp.exp2(m_old - m_new)
                l_ref[hh] = alpha * l_ref[hh] + jnp.sum(p, axis=0, keepdims=True)
                acc_ref[hh] = alpha * acc_ref[hh] + jnp.dot(vTb, p.astype(CDT), preferred_element_type=F32)
            return carry

        def chunk_unshifted(c, carry):
            parts = [(lo, min(MXU_TILE, tk - lo)) for lo in range(0, tk, MXU_TILE)]
            s_cur = [s_ref[lo:lo + n, :] for lo, n in parts]
            for hh in range(group):
                nxt = (c, hh + 1) if hh + 1 < group else (jnp.minimum(c + 1, n_chunks - 1), 0)
                s_nxt = []
                acc = acc_ref[hh]
                l = l_ref[hh]
                for (lo, n), s in zip(parts, s_cur):
                    s_nxt.append(scores(*nxt, lo, n))
                    p = jnp.exp2(s)
                    l = l + jnp.sum(p, axis=0, keepdims=True)
                    acc = acc + jnp.dot(vT_ref[0, c, rows, lo:lo + n], p.astype(CDT), preferred_element_type=F32)
                acc_ref[hh] = acc
                l_ref[hh] = l
                s_cur = s_nxt
            for (lo, n), s in zip(parts, s_cur):
                s_ref[lo:lo + n, :] = s
            return carry

        lax.fori_loop(0, n_chunks, chunk_online if online else chunk_unshifted, 0)
        for hh in range(group):
            h = g * group + hh
            o_ref[0, HEAD_DIM * h:HEAD_DIM * (h + 1), :] = (acc_ref[hh] / l_ref[hh]).astype(o_ref.dtype)


def _gqa_call(qT, k, vT, *, tq, online):
    bsz, _, _, sq = qT.shape
    sk = k.shape[2]
    n_chunks, vr, tk = vT.shape[1:]
    group = A_HEADS // A_KV_HEADS
    return pl.pallas_call(
        functools.partial(_gqa_kernel, online=online),
        grid=(bsz, sq // tq),
        in_specs=[pl.BlockSpec((1, A_HEADS, HEAD_DIM, tq), lambda b, i: (b, 0, 0, i)),
                  pl.BlockSpec((1, A_KV_HEADS, sk, HEAD_DIM), lambda b, i: (b, 0, 0, 0)),
                  pl.BlockSpec((1, n_chunks, vr, tk), lambda b, i: (b, 0, 0, 0))],
        out_specs=pl.BlockSpec((1, A_Q, tq), lambda b, i: (b, 0, i)),
        out_shape=jax.ShapeDtypeStruct((bsz, A_Q, sq), CDT),
        scratch_shapes=[pltpu.VMEM((group, 1, tq), F32), pltpu.VMEM((group, 1, tq), F32),
                        pltpu.VMEM((group, HEAD_DIM, tq), F32), pltpu.VMEM((tk, tq), F32)],
        compiler_params=_cparams("arbitrary", "arbitrary"),
        name="gqa_online" if online else "gqa",
    )(qT, k, vT)


def _key_chunks(vT, tk):
    bsz, vr, sk = vT.shape
    return vT.reshape(bsz, vr, sk // tk, tk).transpose(0, 2, 1, 3)


def _gqa(qT, k, vT, gain_q, gain_k, *, tq):
    bound = (HEAD_DIM ** 0.5 * LOG2E) * jnp.max(jnp.abs(gain_q)) * jnp.max(jnp.abs(gain_k))
    return lax.cond(bound < MAX_UNSHIFTED_LOG2_SCORE,
                    functools.partial(_gqa_call, tq=tq, online=False),
                    functools.partial(_gqa_call, tq=tq, online=True),
                    qT, k, vT)


def _pair_queries(qp):
    lane = lax.broadcasted_iota(jnp.int32, qp.shape, 1)
    zero = jnp.zeros_like(qp)
    return jnp.concatenate([jnp.where(lane < HEAD_DIM, qp, zero), jnp.where(lane >= HEAD_DIM, qp, zero)], axis=0)


def _pair_finish(pv, l):
    n = pv.shape[0] // 2
    pv = pv / l
    lane = lax.broadcasted_iota(jnp.int32, (n, 2 * HEAD_DIM), 1)
    return jnp.where(lane < HEAD_DIM, pv[:n], pv[n:])


def _natten_kernel(q_ref, k_ref, v_ref, kc_ref, vc_ref, bias_ref, o_ref, *, rows_per_step, n_rows):
    i = pl.program_id(1)
    win = WIN_R * GRID_W
    n_pairs = B_HEADS // 2

    def window(rr):
        r = i * rows_per_step + rr
        r0 = jnp.clip(r - WIN_R // 2, 0, n_rows - WIN_R)
        return pl.multiple_of(r0 * GRID_W, GRID_W), r0 - r + (WIN_R - 1)

    def scores(rr, p):
        start, cfg = window(rr)
        cols = slice(2 * HEAD_DIM * p, 2 * HEAD_DIM * (p + 1))
        qs = _pair_queries(q_ref[0, 0, GRID_W * rr:GRID_W * (rr + 1), cols])
        s_loc = _dot_nt(qs, k_ref[0, 0, pl.ds(start, win), cols]) + bias_ref[cfg, p]
        return s_loc, _dot_nt(qs, kc_ref[0, 0, :, cols])

    def finish(rr, p, s_loc, s_ctx):
        start, _ = window(rr)
        cols = slice(2 * HEAD_DIM * p, 2 * HEAD_DIM * (p + 1))
        m = jnp.maximum(jnp.max(s_loc, axis=-1, keepdims=True), jnp.max(s_ctx, axis=-1, keepdims=True))
        e_loc = jnp.exp2(s_loc - m)
        e_ctx = jnp.exp2(s_ctx - m)
        l = jnp.sum(e_loc, axis=-1, keepdims=True) + jnp.sum(e_ctx, axis=-1, keepdims=True)
        pv = _dot(e_loc, v_ref[0, 0, pl.ds(start, win), cols]) + _dot(e_ctx, vc_ref[0, 0, :, cols])
        return _pair_finish(pv, l)

    units = [(rr, p) for rr in range(rows_per_step) for p in range(n_pairs)]
    nxt = scores(*units[0])
    outs = []
    for idx, (rr, p) in enumerate(units):
        cur = nxt
        if idx + 1 < len(units):
            nxt = scores(*units[idx + 1])
        outs.append(finish(rr, p, *cur))
        if p == n_pairs - 1:
            o_ref[0, GRID_W * rr:GRID_W * (rr + 1), :] = jnp.concatenate(outs, axis=-1).astype(o_ref.dtype)
            outs = []


def _natten(proj, cproj, bias, *, rows_per_step):
    bsz, _, s, _ = proj.shape
    lc = cproj.shape[2]
    n_rows = s // GRID_W
    tq = rows_per_step * GRID_W
    whole = lambda blk: (lambda b, i: (b, blk, 0, 0))
    return pl.pallas_call(
        functools.partial(_natten_kernel, rows_per_step=rows_per_step, n_rows=n_rows),
        grid=(bsz, s // tq),
        in_specs=[pl.BlockSpec((1, 1, tq, CB), lambda b, i: (b, P_QB, i, 0)),
                  pl.BlockSpec((1, 1, s, CB), whole(P_KB), pipeline_mode=pl.Buffered(1)),
                  pl.BlockSpec((1, 1, s, CB), whole(P_VB), pipeline_mode=pl.Buffered(1)),
                  pl.BlockSpec((1, 1, lc, CB), whole(P_KB)),
                  pl.BlockSpec((1, 1, lc, CB), whole(P_VB)),
                  pl.BlockSpec(bias.shape, lambda b, i: (0, 0, 0, 0), pipeline_mode=pl.Buffered(1))],
        out_specs=pl.BlockSpec((1, tq, B_W), lambda b, i: (b, i, 0)),
        out_shape=jax.ShapeDtypeStruct((bsz, s, B_W), CDT),
        compiler_params=_cparams("arbitrary", "arbitrary"),
        name="natten",
    )(proj, proj, proj, cproj, cproj, bias)


def _ctx_attn_b_kernel(q_ref, k_ref, v_ref, o_ref):
    outs = []
    for p in range(B_HEADS // 2):
        cols = slice(2 * HEAD_DIM * p, 2 * HEAD_DIM * (p + 1))
        qs = _pair_queries(q_ref[0, 0, :, cols])
        s = _dot_nt(qs, k_ref[0, 0, :, cols])
        e = jnp.exp2(s - jnp.max(s, axis=-1, keepdims=True))
        outs.append(_pair_finish(_dot(e, v_ref[0, 0, :, cols]), jnp.sum(e, axis=-1, keepdims=True)))
    o_ref[0] = jnp.concatenate(outs, axis=-1).astype(o_ref.dtype)


def _ctx_attn_b(cproj):
    bsz, _, lc, _ = cproj.shape
    blk = lambda c: pl.BlockSpec((1, 1, lc, CB), lambda b: (b, c, 0, 0))
    return pl.pallas_call(
        _ctx_attn_b_kernel,
        grid=(bsz,),
        in_specs=[blk(P_QB), blk(P_KB), blk(P_VB)],
        out_specs=pl.BlockSpec((1, lc, B_W), lambda b: (b, 0, 0)),
        out_shape=jax.ShapeDtypeStruct((bsz, lc, B_W), CDT),
        compiler_params=_cparams("arbitrary"),
        name="ctx_attn_b",
    )(cproj, cproj, cproj)


def _natten_bias(rpb):
    h = rpb.shape[0]
    nd = 2 * WIN_C - 1
    lead = GRID_W - WIN_C
    w = jnp.pad(rpb.astype(F32), ((0, 0), (0, 0), (lead, 2 * GRID_W - lead - nd)))
    t = jnp.tile(w, (1, 1, GRID_W))[..., :GRID_W * (2 * GRID_W - 1)]
    toe = t.reshape(h, 2 * WIN_R - 1, GRID_W, 2 * GRID_W - 1)[..., GRID_W - 1:]
    qc = jnp.arange(GRID_W)
    c0 = jnp.clip(qc - WIN_C // 2, 0, GRID_W - WIN_C)
    inwin = (qc[None, :] >= c0[:, None]) & (qc[None, :] < c0[:, None] + WIN_C)
    toe = jnp.where(inwin, toe * LOG2E, -1e30)
    toe = toe.transpose(0, 2, 1, 3)
    tab = jnp.stack([toe[:, :, cfg:cfg + WIN_R] for cfg in range(WIN_R)])
    return tab.reshape(WIN_R, h // 2, 2 * GRID_W, WIN_R * GRID_W)


def _gdn_feat_kernel(x_ref, p_ref, n_ref, w_ref, o_ref, xs_ref):
    i = pl.program_id(1)
    c = pl.program_id(2)
    tm = x_ref.shape[2]
    hp = BF16_SUBLANES
    e = F32_SUBLANES
    xs_ref[0:e] = jnp.where(i > 0, p_ref[0, 0].astype(F32), 0.0)[hp - e:]
    xs_ref[e:e + tm] = x_ref[0, 0].astype(F32)
    xs_ref[e + tm:] = jnp.where(i < pl.num_programs(1) - 1, n_ref[0, 0].astype(F32), 0.0)[:e]
    w = w_ref[...]
    half = CONV_K // 2
    y = w[0:1] * xs_ref[e - half:e - half + tm]
    for j in range(1, CONV_K):
        y = y + w[j:j + 1] * xs_ref[e - half + j:e - half + j + tm]
    y = _silu(y)

    @pl.when(c == 2)
    def _():
        o_ref[0] = y.astype(o_ref.dtype)

    @pl.when(c < 2)
    def _():
        scale = jnp.where(c == 0, C_DK ** -0.5, 1.0)
        parts = []
        for h in range(C_HEADS):
            yh = y[:, C_DK * h:C_DK * (h + 1)]
            parts.append(yh * (lax.rsqrt(jnp.sum(yh * yh, axis=-1, keepdims=True) + EPS) * scale))
        o_ref[0] = jnp.concatenate(parts, axis=-1).astype(o_ref.dtype)


def _gdn_features(proj, conv_w, *, tm):
    bsz, _, s, _ = proj.shape
    hp = BF16_SUBLANES
    nh = s // hp
    return pl.pallas_call(
        _gdn_feat_kernel,
        grid=(bsz, s // tm, 3),
        in_specs=[pl.BlockSpec((1, 1, tm, CB), lambda b, i, c: (b, P_GDN0 + c, i, 0)),
                  pl.BlockSpec((1, 1, hp, CB),
                               lambda b, i, c: (b, P_GDN0 + c, jnp.maximum(i * (tm // hp) - 1, 0), 0)),
                  pl.BlockSpec((1, 1, hp, CB),
                               lambda b, i, c: (b, P_GDN0 + c, jnp.minimum((i + 1) * (tm // hp), nh - 1), 0)),
                  pl.BlockSpec((CONV_K, CB), lambda b, i, c: (0, c))],
        out_specs=pl.BlockSpec((1, tm, CB), lambda b, i, c: (b, i, c)),
        out_shape=jax.ShapeDtypeStruct((bsz, s, 3 * CB), CDT),
        scratch_shapes=[pltpu.VMEM((tm + 2 * F32_SUBLANES, CB), F32)],
        compiler_params=_cparams("arbitrary", "arbitrary", "arbitrary"),
        name="gdn_features",
    )(proj, proj, proj, conv_w)


def _block_diag(x, n, blk):
    lane_blk = lax.broadcasted_iota(jnp.int32, (1, n * blk), 1) // blk
    return jnp.concatenate([jnp.where(lane_blk == h, x, 0.0) for h in range(n)], axis=0)


def _gdn_prepare(d, feat, bgc, bgr):
    nh = C_HEADS
    col0 = d * nh
    last = CHUNK - 1 if d == 0 else 0
    q = feat[:, :C_K].astype(F32)
    k = feat[:, C_K:2 * C_K].astype(F32)
    v = feat[:, 2 * C_K:].astype(F32)

    def wide(c0, width):
        return jnp.concatenate([jnp.broadcast_to(bgc[:, c0 + h:c0 + h + 1], (CHUNK, width)) for h in range(nh)],
                               axis=1)

    beta = wide(col0, C_DK)
    gcol = wide(N_GATE_DIRS + col0, C_DK)
    gcol_c = wide(N_GATE_DIRS + col0, CHUNK)
    grow_c = jnp.concatenate([bgr[N_GATE_DIRS + col0 + h:N_GATE_DIRS + col0 + h + 1, :] for h in range(nh)], axis=1)
    glast = gcol[last:last + 1, :]
    egc = jnp.exp(gcol)
    kb = k * beta
    ii = lax.broadcasted_iota(jnp.int32, (CHUNK, nh * CHUNK), 0)
    jj = lax.broadcasted_iota(jnp.int32, (CHUNK, nh * CHUNK), 1) % CHUNK
    incl = (ii >= jj) if d == 0 else (ii <= jj)
    strict = (ii > jj) if d == 0 else (ii < jj)
    r = _dot_nt(jnp.concatenate([kb, q], axis=0), _block_diag(k, nh, C_DK))
    decay = jnp.exp(jnp.where(incl, gcol_c - grow_c, -jnp.inf))
    nil = jnp.where(strict, -r[:CHUNK] * decay, 0.0)
    rhs_w = jnp.concatenate(
        [jnp.concatenate([(v * beta)[:, C_DV * h:C_DV * (h + 1)], (kb * egc)[:, C_DK * h:C_DK * (h + 1)]], axis=1)
         for h in range(nh)], axis=0).astype(CDT)
    k_dec = k * jnp.exp(glast - gcol)
    return dict(
        tinv=jnp.where(ii == jj, 1.0, 0.0) + nil,
        power=nil,
        aqk=r[CHUNK:] * decay,
        rhs_w=rhs_w,
        qd=(q * egc).astype(CDT),
        k_decT=[k_dec[:, C_DK * h:C_DK * (h + 1)].T.astype(CDT) for h in range(nh)],
        eglast=jnp.exp(glast),
    )


def _gdn_scan_kernel(ff_ref, fr_ref, bgf_ref, bgr_ref, bgTf_ref, bgTr_ref, s0_ref,
                     of_ref, or_ref, sN_ref, s_ref, *, n_chunks):
    step = pl.program_id(0)
    bsz = ff_ref.shape[0]
    nh = C_HEADS
    io = ((ff_ref, bgf_ref, bgTf_ref, of_ref), (fr_ref, bgr_ref, bgTr_ref, or_ref))

    @pl.when(step == 0)
    def _():
        s_ref[...] = s0_ref[...]

    units = {}
    for c in range(n_chunks):
        rows = slice(CHUNK * c, CHUNK * (c + 1))
        for b in range(bsz):
            for d in range(2):
                f_ref, bg_ref, bgT_ref, _ = io[d]
                units[b, d, c] = _gdn_prepare(d, f_ref[b, rows, :], bg_ref[b, rows, :], bgT_ref[b, :, rows])
    us = list(units.values())

    bd = lambda x: _block_diag(x, nh, CHUNK)
    n_sq = int(math.log2(CHUNK)) - 1
    for u in us:
        u["power"] = _dot(u["power"], bd(u["power"]))
    for _ in range(n_sq - 1):
        for u in us:
            r = _dot(jnp.concatenate([u["tinv"], u["power"]], axis=0), bd(u["power"]))
            u["tinv"] = u["tinv"] + r[:CHUNK]
            u["power"] = r[CHUNK:]
    for u in us:
        u["tinv"] = u["tinv"] + _dot(u["tinv"], bd(u["power"]))
    for u in us:
        w = _dot(bd(u["tinv"]), u["rhs_w"])
        u["w_v"] = [w[CHUNK * h:CHUNK * (h + 1), :C_DV] for h in range(nh)]
        u["k_cd"] = [w[CHUNK * h:CHUNK * (h + 1), C_DV:].astype(CDT) for h in range(nh)]

    state = {(b, d, h): s_ref[b, d, h] for b in range(bsz) for d in range(2) for h in range(nh)}
    for t in range(n_chunks):
        act = [(b, d, t if d == 0 else n_chunks - 1 - t) for b in range(bsz) for d in range(2)]
        v_new, q_state = {}, {}
        for key in act:
            u = units[key]
            for h in range(nh):
                x = _dot(jnp.concatenate([u["k_cd"][h], u["qd"][:, C_DK * h:C_DK * (h + 1)]], axis=0),
                         state[key[0], key[1], h])
                v_new[key, h] = u["w_v"][h] - x[:CHUNK]
                q_state[key, h] = x[CHUNK:]
        for key in act:
            u = units[key]
            for h in range(nh):
                sk = (key[0], key[1], h)
                state[sk] = (state[sk] * u["eglast"][:, C_DV * h:C_DV * (h + 1)]
                             + _dot(u["k_decT"][h], v_new[key, h]))
        for key in act:
            b, d, c = key
            av = _dot(bd(units[key]["aqk"]), jnp.concatenate([v_new[key, h] for h in range(nh)], axis=0))
            for h in range(nh):
                o_ref = io[d][3]
                o_ref[b, CHUNK * c:CHUNK * (c + 1), C_DV * h:C_DV * (h + 1)] = (
                    q_state[key, h] + av[CHUNK * h:CHUNK * (h + 1)]).astype(o_ref.dtype)
    for (b, d, h), val in state.items():
        s_ref[b, d, h] = val

    @pl.when(step == pl.num_programs(0) - 1)
    def _():
        sN_ref[...] = s_ref[...]


def _gdn_scan(feat, bg, bgT, state0, *, n_chunks):
    bsz, s, _ = feat.shape
    tb = n_chunks * CHUNK
    ns = s // tb
    fwd = lambda i: (0, i, 0)
    rev = lambda i: (0, ns - 1 - i, 0)
    fwdT = lambda i: (0, 0, i)
    revT = lambda i: (0, 0, ns - 1 - i)
    st = lambda i: (0, 0, 0, 0, 0)
    st_shape = (bsz, 2, C_HEADS, C_DK, C_DV)
    return pl.pallas_call(
        functools.partial(_gdn_scan_kernel, n_chunks=n_chunks),
        grid=(ns,),
        in_specs=[pl.BlockSpec((bsz, tb, 3 * CB), fwd), pl.BlockSpec((bsz, tb, 3 * CB), rev),
                  pl.BlockSpec((bsz, tb, LANES), fwd), pl.BlockSpec((bsz, tb, LANES), rev),
                  pl.BlockSpec((bsz, 2 * N_GATE_DIRS, tb), fwdT), pl.BlockSpec((bsz, 2 * N_GATE_DIRS, tb), revT),
                  pl.BlockSpec(st_shape, st)],
        out_specs=[pl.BlockSpec((bsz, tb, C_V), fwd), pl.BlockSpec((bsz, tb, C_V), rev),
                   pl.BlockSpec(st_shape, st)],
        out_shape=[jax.ShapeDtypeStruct((bsz, s, C_V), F32), jax.ShapeDtypeStruct((bsz, s, C_V), F32),
                   jax.ShapeDtypeStruct(st_shape, F32)],
        scratch_shapes=[pltpu.VMEM(st_shape, F32)],
        compiler_params=_cparams("arbitrary"),
        name="gdn_scan",
    )(feat, feat, bg, bg, bgT, bgT, state0)


def _merge_kernel(x_ref, oaT_ref, ob_ref, of_ref, or_ref, z_ref, g0_ref, g1_ref, g2_ref, onorm_ref,
                  wb_ref, wo_ref, gt_ref, out_ref, *, row_mul, row_off):
    b = pl.program_id(0)
    oc = of_ref[0].astype(F32) + or_ref[0].astype(F32)
    parts = []
    for h in range(C_HEADS):
        oh = oc[:, C_DV * h:C_DV * (h + 1)]
        parts.append(oh * lax.rsqrt(jnp.mean(oh * oh, axis=-1, keepdims=True) + EPS) * onorm_ref[...])
    yc = jnp.concatenate(parts, axis=-1) * _silu(z_ref[0, 0].astype(F32))
    gate = lambda g_ref: _sigmoid(jnp.concatenate([g_ref[0, 0], g_ref[0, 1]], axis=-1).astype(F32))
    m = (gate(g0_ref) * _dot_tn(oaT_ref[0], wb_ref[0, 0])
         + gate(g1_ref) * _dot(ob_ref[0], wb_ref[0, 1])
         + gate(g2_ref) * _dot(yc, wb_ref[0, 2]))
    gt = gt_ref[pl.ds(b * row_mul + row_off, 1), :]
    out_ref[0] = x_ref[0] + gt * _dot(m, wo_ref[0])


def _merge(x, oaT, ob, of, orr, proj, onorm, wb, wo, layer, mod_l, *, row_mul, row_off, tm):
    bsz, s, d = x.shape
    assert d == 2 * CB
    tok = lambda blk: (lambda b, i: (b, i, blk))
    gate = lambda n: pl.BlockSpec((1, 2, tm, CB), lambda b, i: (b, n, i, 0))
    return pl.pallas_call(
        functools.partial(_merge_kernel, row_mul=row_mul, row_off=row_off),
        grid=(bsz, s // tm),
        in_specs=[pl.BlockSpec((1, tm, d), tok(0)),
                  pl.BlockSpec((1, A_Q, tm), lambda b, i: (b, 0, i)),
                  pl.BlockSpec((1, tm, B_W), tok(0)),
                  pl.BlockSpec((1, tm, C_V), tok(0)),
                  pl.BlockSpec((1, tm, C_V), tok(0)),
                  pl.BlockSpec((1, 1, tm, CB), lambda b, i: (b, P_Z, i, 0)),
                  gate(P_GATE_PAIRS[0]),
                  gate(P_GATE_PAIRS[1]),
                  gate(P_GATE_PAIRS[2]),
                  pl.BlockSpec((1, C_DV), lambda b, i: (0, 0)),
                  pl.BlockSpec((1,) + wb.shape[1:], lambda b, i: (layer, 0, 0, 0)),
                  pl.BlockSpec((1,) + wo.shape[1:], lambda b, i: (layer, 0, 0)),
                  pl.BlockSpec((8, d), lambda b, i: (0, 2))],
        out_specs=pl.BlockSpec((1, tm, d), tok(0)),
        out_shape=jax.ShapeDtypeStruct((bsz, s, d), F32),
        compiler_params=_cparams("arbitrary", "arbitrary"),
        name="merge",
    )(x, oaT, ob, of, orr, proj, proj, proj, proj, onorm.reshape(1, C_DV), wb, wo, mod_l)


def _mlp_kernel(x_ref, sh_ref, sc_ref, gt_ref, gain_ref, w1_ref, w2_ref, fin_ref, out_ref, h_ref, acc_ref,
                *, row_mul, row_off, final_norm):
    b = pl.program_id(0)
    f = pl.program_id(2)
    row = b * row_mul + row_off

    @pl.when(f == 0)
    def _():
        x = x_ref[0]
        r = lax.rsqrt(jnp.mean(x * x, axis=-1, keepdims=True) + EPS)
        h = (x * r * gain_ref[...]) * (1.0 + sc_ref[pl.ds(row, 1), :]) + sh_ref[pl.ds(row, 1), :]
        h_ref[...] = h.astype(CDT)
        acc_ref[...] = jnp.zeros_like(acc_ref)

    a = jnp.maximum(jnp.dot(h_ref[...], w1_ref[0], preferred_element_type=F32), 0.0)
    acc_ref[...] += _dot(a * a, w2_ref[0])

    @pl.when(f == pl.num_programs(2) - 1)
    def _():
        y = x_ref[0] + gt_ref[pl.ds(row, 1), :] * acc_ref[...]
        if final_norm:
            y = y * lax.rsqrt(jnp.mean(y * y, axis=-1, keepdims=True) + EPS) * fin_ref[...]
        out_ref[0] = y


def _mlp(x, mod_l, gain, w1, w2, layer, fin, *, row_mul, row_off, tm, tf, final_norm):
    bsz, s, d = x.shape
    dff = w1.shape[2]
    return pl.pallas_call(
        functools.partial(_mlp_kernel, row_mul=row_mul, row_off=row_off, final_norm=final_norm),
        grid=(bsz, s // tm, dff // tf),
        in_specs=[pl.BlockSpec((1, tm, d), lambda b, i, f: (b, i, 0)),
                  pl.BlockSpec((8, d), lambda b, i, f: (0, 3)),
                  pl.BlockSpec((8, d), lambda b, i, f: (0, 4)),
                  pl.BlockSpec((8, d), lambda b, i, f: (0, 5)),
                  pl.BlockSpec((1, d), lambda b, i, f: (0, 0)),
                  pl.BlockSpec((1, d, tf), lambda b, i, f: (layer, 0, f)),
                  pl.BlockSpec((1, tf, d), lambda b, i, f: (layer, f, 0)),
                  pl.BlockSpec((1, d), lambda b, i, f: (0, 0))],
        out_specs=pl.BlockSpec((1, tm, d), lambda b, i, f: (b, i, 0)),
        out_shape=jax.ShapeDtypeStruct((bsz, s, d), F32),
        scratch_shapes=[pltpu.VMEM((tm, d), CDT), pltpu.VMEM((tm, d), F32)],
        compiler_params=_cparams("arbitrary", "arbitrary", "arbitrary"),
        name="mlp",
    )(x, mod_l, mod_l, mod_l, gain.reshape(1, d), w1, w2, fin.reshape(1, d))


def _split_w_in(w_in):
    w = w_in.astype(CDT)
    small = 2 * N_GATE_DIRS
    pad = jnp.zeros(w.shape[:-1] + (W_TAIL_GATE - small,), CDT)
    return w, jnp.concatenate([w[..., W_MAIN:W_MAIN + small], pad, w[..., W_MAIN + small:]], axis=-1)


def _rope_tables(n_tok, gain_q, gain_k, rotate):
    if rotate:
        t = jnp.arange(n_tok)
        row = (t // GRID_W).astype(F32)
        col = (t % GRID_W).astype(F32)
        axis_dim = HEAD_DIM // 2
        freqs = ROPE_BASE ** (-jnp.arange(0, axis_dim, 2, dtype=F32) / axis_dim)
        ang = jnp.concatenate([row[:, None] * freqs, col[:, None] * freqs], axis=-1)
        cos = jnp.repeat(jnp.cos(ang), 2, axis=-1).T
        sin = jnp.repeat(jnp.sin(ang), 2, axis=-1).T
    else:
        cos = jnp.ones((HEAD_DIM, n_tok), F32)
        sin = jnp.zeros((HEAD_DIM, n_tok), F32)
    even = (jnp.arange(HEAD_DIM) % 2 == 0)[:, None]

    def tables(g, scale):
        g = g.astype(F32) * scale
        return [cos * g[:, None],
                jnp.where(even, -sin * jnp.roll(g, -1)[:, None], 0.0),
                jnp.where(even, 0.0, sin * jnp.roll(g, 1)[:, None])]

    return jnp.stack(tables(gain_q, HEAD_DIM ** -0.5 * LOG2E) + tables(gain_k, 1.0))


def _decay_params(a_log, dt_bias):
    row = lambda v: jnp.zeros((LANES,), F32).at[N_GATE_DIRS:2 * N_GATE_DIRS].set(v.reshape(-1).astype(F32))
    return jnp.zeros((8, LANES), F32).at[0].set(row(a_log)).at[1].set(row(dt_bias))


def kernel(x, c, ctx, c_ctx, w_mod, b_mod, norm_mix, w_in, q_norm_a, k_norm_a, rpb_b, conv_c, a_log_c, dt_bias_c,
           o_norm_c, w_branch, w_out, norm_ffn, w_ffn1, w_ffn2, norm_final):
    bsz, s, d = x.shape
    lc = ctx.shape[1]
    depth = w_mod.shape[0]
    assert bsz < 8 and s % 1024 == 0 and lc == 256 and d % CB == 0

    tm = 1024
    tk = next(t for t in (768, 640, 512, 384, 256, 128) if (s + lc) % t == 0)
    cc = jnp.zeros((8, d), F32).at[:bsz].set(c).at[bsz].set(c_ctx)
    mod = _modulation(cc, w_mod, b_mod)
    w_all, w_tail = _split_w_in(w_in)
    wb = w_branch.astype(CDT)
    wo = w_out.astype(CDT)
    w1 = w_ffn1.astype(CDT)
    w2 = w_ffn2.astype(CDT)
    lat = dict(row_mul=1, row_off=0)
    con = dict(row_mul=0, row_off=bsz)
    zero_state = jnp.zeros((bsz, 2, C_HEADS, C_DK, C_DV), F32)

    xc = ctx
    for l in range(depth):
        need_ctx = l < depth - 1
        par = _decay_params(a_log_c[l], dt_bias_c[l])
        rope_l = _rope_tables(s, q_norm_a[l], k_norm_a[l], True)
        rope_c = _rope_tables(lc, q_norm_a[l], k_norm_a[l], False)
        bias = _natten_bias(rpb_b[l])
        conv_w = conv_c[l].astype(F32)

        P = _inproj(x, mod[l], norm_mix[l], w_all, w_tail, l, rope_l, par, tm=512, **lat)
        Pc = _inproj(xc, mod[l], norm_mix[l], w_all, w_tail, l, rope_c, par, tm=lc, **con)

        k_all = jnp.concatenate([P["k"], Pc["k"]], axis=2)
        vT_all = _key_chunks(jnp.concatenate([P["vT"], Pc["vT"]], axis=2), tk)
        o_aT = _gqa(P["qT"], k_all, vT_all, q_norm_a[l], k_norm_a[l], tq=512)
        o_b = _natten(P["proj"], Pc["proj"], bias, rows_per_step=8)
        featc = _gdn_features(Pc["proj"], conv_w, tm=lc)
        feat = _gdn_features(P["proj"], conv_w, tm=512)
        oc_f, oc_r, state = _gdn_scan(featc, Pc["bg"], Pc["bgT"], zero_state, n_chunks=2)
        o_f, o_r, _ = _gdn_scan(feat, P["bg"], P["bgT"], state, n_chunks=4)

        last = l == depth - 1
        x = _merge(x, o_aT, o_b, o_f, o_r, P["proj"], o_norm_c[l], wb, wo, l, mod[l], tm=tm, **lat)
        x = _mlp(x, mod[l], norm_ffn[l], w1, w2, l, norm_final, tm=tm, tf=1024, final_norm=last, **lat)
        if need_ctx:
            oc_aT = _gqa_call(Pc["qT"], Pc["k"], _key_chunks(Pc["vT"], lc), tq=lc, online=True)
            oc_b = _ctx_attn_b(Pc["proj"])
            xc = _merge(xc, oc_aT, oc_b, oc_f, oc_r, Pc["proj"], o_norm_c[l], wb, wo, l, mod[l], tm=lc, **con)
            xc = _mlp(xc, mod[l], norm_ffn[l], w1, w2, l, norm_final, tm=lc, tf=1024, final_norm=False, **con)
    return x
```

```python
import functools
import math

import jax
import jax.numpy as jnp
from jax import lax
from jax.experimental import pallas as pl
from jax.experimental.pallas import tpu as pltpu

F32 = jnp.float32
CDT = jnp.bfloat16

GRID_W = 64
HEAD_DIM = 64
A_HEADS = 8
A_KV_HEADS = 2
B_HEADS = 8
WIN_R = 8
WIN_C = 16
C_HEADS = 4
C_DK = 128
C_DV = 128
CONV_K = 5
CHUNK = 64
N_BRANCH = 3
ROPE_BASE = 10000.0
EPS = 1e-6

A_Q = A_HEADS * HEAD_DIM
A_KV = A_KV_HEADS * HEAD_DIM
B_W = B_HEADS * HEAD_DIM
C_K = C_HEADS * C_DK
C_V = C_HEADS * C_DV
N_GATE_DIRS = 2 * C_HEADS

LANES = 128
MXU_TILE = 256
F32_SUBLANES = 8
BF16_SUBLANES = 16
VMEM_LIMIT = 56 * 1024 * 1024

MAX_UNSHIFTED_LOG2_SCORE = 100.0

W_QA = 0
W_KA = W_QA + A_Q
W_QB = W_KA + 2 * A_KV
W_KB = W_QB + B_W
W_VB = W_KB + B_W
W_GDN = W_VB + B_W
W_Z = W_GDN + 2 * C_K + C_V
W_MAIN = W_Z + C_V
W_TAIL_GATE = LANES

CB = 512
PROJ_SRC = (tuple((True, W_TAIL_GATE + CB * j) for j in range(6))
            + ((False, W_QB), (False, W_KB), (False, W_VB))
            + tuple((False, W_GDN + CB * j) for j in range(3)) + ((False, W_Z),))
N_PBLK = len(PROJ_SRC)
P_GATE_PAIRS = (0, 1, 2)
P_QB, P_KB, P_VB, P_GDN0, P_Z = 6, 7, 8, 9, 12

LOG2E = math.log2(math.e)
QB_SCALE = HEAD_DIM ** -0.5 * LOG2E


def _cparams(*sem):
    return pltpu.CompilerParams(dimension_semantics=sem, vmem_limit_bytes=VMEM_LIMIT)


def _dot(a, b):
    return jnp.dot(a.astype(CDT), b.astype(CDT), preferred_element_type=F32)


def _dot_nt(a, b):
    return lax.dot_general(a.astype(CDT), b.astype(CDT), (((1,), (1,)), ((), ())), preferred_element_type=F32)


def _dot_tn(a, b):
    return lax.dot_general(a.astype(CDT), b.astype(CDT), (((0,), (0,)), ((), ())), preferred_element_type=F32)


def _sigmoid(x):
    return 1.0 / (1.0 + jnp.exp(-x))


def _silu(x):
    return x * _sigmoid(x)


def _mod_kernel(c_ref, w_ref, b_ref, o_ref):
    o_ref[0] = _dot(_silu(c_ref[...]), w_ref[0]) + b_ref[0]


def _modulation(cc, w_mod, b_mod):
    depth, d, n = w_mod.shape
    tn = 1024
    return pl.pallas_call(
        _mod_kernel,
        grid=(depth, n // tn),
        in_specs=[pl.BlockSpec((8, d), lambda l, j: (0, 0)),
                  pl.BlockSpec((1, d, tn), lambda l, j: (l, 0, j)),
                  pl.BlockSpec((1, 1, tn), lambda l, j: (l, 0, j))],
        out_specs=pl.BlockSpec((1, 8, tn), lambda l, j: (l, 0, j)),
        out_shape=jax.ShapeDtypeStruct((depth, 8, n), F32),
        compiler_params=_cparams("arbitrary", "arbitrary"),
        name="modulation",
    )(cc, w_mod, b_mod.reshape(depth, 1, n))


def _head_norm_rope(xT, c, sa, sb):
    n = xT * lax.rsqrt(jnp.mean(xT * xT, axis=0, keepdims=True) + EPS)
    return n * c + pltpu.roll(n, HEAD_DIM - 1, 0) * sa + pltpu.roll(n, 1, 0) * sb


def _inproj_kernel(x_ref, sh_ref, sc_ref, gain_ref, w_ref, wt_ref, rope_ref, par_ref,
                   qT_ref, k_ref, vT_ref, bg_ref, bgT_ref, proj_ref, h_ref, *, row_mul, row_off):
    b = pl.program_id(0)
    tm = x_ref.shape[1]
    x = x_ref[0]
    r = lax.rsqrt(jnp.mean(x * x, axis=-1, keepdims=True) + EPS)
    row = b * row_mul + row_off
    h_ref[...] = ((x * r * gain_ref[...]) * (1.0 + sc_ref[pl.ds(row, 1), :]) + sh_ref[pl.ds(row, 1), :]).astype(CDT)

    def mm(ref, lo, n):
        return jnp.dot(h_ref[...], ref[0, :, lo:lo + n], preferred_element_type=F32)

    t = mm(w_ref, W_QA, A_Q).T
    for h in range(A_HEADS):
        o = _head_norm_rope(t[HEAD_DIM * h:HEAD_DIM * (h + 1)], rope_ref[0], rope_ref[1], rope_ref[2])
        qT_ref[0, h] = o.astype(CDT)

    acc = mm(w_ref, W_KA, 2 * A_KV)
    t = acc[:, :A_KV].T
    for g in range(A_KV_HEADS):
        kT = _head_norm_rope(t[HEAD_DIM * g:HEAD_DIM * (g + 1)], rope_ref[3], rope_ref[4], rope_ref[5])
        k_ref[0, g] = kT.T.astype(CDT)
    vT_ref[0] = acc[:, A_KV:].T.astype(CDT)

    raw = mm(wt_ref, 0, LANES)
    lane = lax.broadcasted_iota(jnp.int32, (tm, LANES), 1)
    rowi = lax.broadcasted_iota(jnp.int32, (tm, LANES), 0) % CHUNK
    beta = _sigmoid(raw)
    y = raw + par_ref[1:2, :]
    g = -jnp.exp(par_ref[0:1, :]) * (jnp.maximum(y, 0.0) + jnp.log1p(jnp.exp(-jnp.abs(y))))
    fwd = g
    rev = g
    for s in (1, 2, 4, 8, 16, 32):
        fwd = fwd + jnp.where(rowi >= s, pltpu.roll(fwd, s, 0), 0.0)
        rev = rev + jnp.where(rowi < CHUNK - s, pltpu.roll(rev, tm - s, 0), 0.0)
    is_rev = (lane >= N_GATE_DIRS + C_HEADS) & (lane < 2 * N_GATE_DIRS)
    bg = jnp.where(lane < N_GATE_DIRS, beta, jnp.where(is_rev, rev, fwd))
    bg = jnp.where(lane < 2 * N_GATE_DIRS, bg, 0.0)
    bg_ref[0] = bg
    bgT_ref[0] = bg.T[:2 * N_GATE_DIRS]

    for jj, (in_tail, lo) in enumerate(PROJ_SRC):
        acc = mm(wt_ref if in_tail else w_ref, lo, CB)
        proj_ref[0, jj] = (acc * QB_SCALE if jj == P_QB else acc).astype(CDT)


def _inproj(x, mod_l, gain, w_all, w_tail, layer, rope, par, *, row_mul, row_off, tm):
    bsz, s, d = x.shape
    kern = functools.partial(_inproj_kernel, row_mul=row_mul, row_off=row_off)
    outs = pl.pallas_call(
        kern,
        grid=(bsz, s // tm),
        in_specs=[pl.BlockSpec((1, tm, d), lambda b, i: (b, i, 0)),
                  pl.BlockSpec((8, d), lambda b, i: (0, 0)),
                  pl.BlockSpec((8, d), lambda b, i: (0, 1)),
                  pl.BlockSpec((1, d), lambda b, i: (0, 0)),
                  pl.BlockSpec((1, d, W_MAIN), lambda b, i: (layer, 0, 0), pipeline_mode=pl.Buffered(1)),
                  pl.BlockSpec((1,) + w_tail.shape[1:], lambda b, i: (layer, 0, 0), pipeline_mode=pl.Buffered(1)),
                  pl.BlockSpec((6, HEAD_DIM, tm), lambda b, i: (0, 0, i)),
                  pl.BlockSpec((8, LANES), lambda b, i: (0, 0))],
        out_specs=[pl.BlockSpec((1, A_HEADS, HEAD_DIM, tm), lambda b, i: (b, 0, 0, i)),
                   pl.BlockSpec((1, A_KV_HEADS, tm, HEAD_DIM), lambda b, i: (b, 0, i, 0)),
                   pl.BlockSpec((1, A_KV, tm), lambda b, i: (b, 0, i)),
                   pl.BlockSpec((1, tm, LANES), lambda b, i: (b, i, 0)),
                   pl.BlockSpec((1, 2 * N_GATE_DIRS, tm), lambda b, i: (b, 0, i)),
                   pl.BlockSpec((1, N_PBLK, tm, CB), lambda b, i: (b, 0, i, 0))],
        out_shape=[jax.ShapeDtypeStruct((bsz, A_HEADS, HEAD_DIM, s), CDT),
                   jax.ShapeDtypeStruct((bsz, A_KV_HEADS, s, HEAD_DIM), CDT),
                   jax.ShapeDtypeStruct((bsz, A_KV, s), CDT),
                   jax.ShapeDtypeStruct((bsz, s, LANES), F32),
                   jax.ShapeDtypeStruct((bsz, 2 * N_GATE_DIRS, s), F32),
                   jax.ShapeDtypeStruct((bsz, N_PBLK, s, CB), CDT)],
        scratch_shapes=[pltpu.VMEM((tm, d), CDT)],
        compiler_params=_cparams("arbitrary", "arbitrary"),
        name="inproj",
    )(x, mod_l, mod_l, gain.reshape(1, d), w_all, w_tail, rope, par)
    return dict(zip(("qT", "k", "vT", "bg", "bgT", "proj"), outs))


def _gqa_kernel(q_ref, k_ref, vT_ref, o_ref, m_ref, l_ref, acc_ref, s_ref, *, online):
    n_chunks, _, tk = vT_ref.shape[1:]
    group = A_HEADS // A_KV_HEADS
    for g in range(A_KV_HEADS):
        rows = slice(g * HEAD_DIM, (g + 1) * HEAD_DIM)
        acc_ref[...] = jnp.zeros(acc_ref.shape, F32)
        l_ref[...] = jnp.zeros(l_ref.shape, F32)
        if online:
            m_ref[...] = jnp.full(m_ref.shape, -jnp.inf, F32)

        def scores(c, hh, lo, n):
            start = pl.multiple_of(c * tk, tk)
            return jnp.dot(k_ref[0, g, pl.ds(start + lo, n), :], q_ref[0, g * group + hh],
                           preferred_element_type=F32)

        s_ref[...] = scores(0, 0, 0, tk)

        def chunk_online(c, carry):
            vTb = vT_ref[0, c, rows, :]
            s_next = s_ref[...]
            for hh in range(group):
                s = s_next
                if hh + 1 < group:
                    s_next = scores(c, hh + 1, 0, tk)
                else:
                    s_ref[...] = scores(jnp.minimum(c + 1, n_chunks - 1), 0, 0, tk)
                m_old = m_ref[hh]
                m_new = jnp.maximum(m_old, jnp.max(s, axis=0, keepdims=True))
                m_ref[hh] = m_new
                p = jnp.exp2(s - m_new)
                alpha = jnp.exp2(m_old - m_new)
                l_ref[hh] = alpha * l_ref[hh] + jnp.sum(p, axis=0, keepdims=True)
                acc_ref[hh] = alpha * acc_ref[hh] + jnp.dot(vTb, p.astype(CDT), preferred_element_type=F32)
            return carry

        def chunk_unshifted(c, carry):
            parts = [(lo, min(MXU_TILE, tk - lo)) for lo in range(0, tk, MXU_TILE)]
            s_cur = [s_ref[lo:lo + n, :] for lo, n in parts]
            for hh in range(group):
                nxt = (c, hh + 1) if hh + 1 < group else (jnp.minimum(c + 1, n_chunks - 1), 0)
                s_nxt = []
                acc = acc_ref[hh]
                l = l_ref[hh]
                for (lo, n), s in zip(parts, s_cur):
                    s_nxt.append(scores(*nxt, lo, n))
                    p = jnp.exp2(s)
                    l = l + jnp.sum(p, axis=0, keepdims=True)
                    acc = acc + jnp.dot(vT_ref[0, c, rows, lo:lo + n], p.astype(CDT), preferred_element_type=F32)
                acc_ref[hh] = acc
                l_ref[hh] = l
                s_cur = s_nxt
            for (lo, n), s in zip(parts, s_cur):
                s_ref[lo:lo + n, :] = s
            return carry

        lax.fori_loop(0, n_chunks, chunk_online if online else chunk_unshifted, 0)
        for hh in range(group):
            h = g * group + hh
            o_ref[0, HEAD_DIM * h:HEAD_DIM * (h + 1), :] = (acc_ref[hh] / l_ref[hh]).astype(o_ref.dtype)


def _gqa_call(qT, k, vT, *, tq, online):
    bsz, _, _, sq = qT.shape
    sk = k.shape[2]
    n_chunks, vr, tk = vT.shape[1:]
    group = A_HEADS // A_KV_HEADS
    return pl.pallas_call(
        functools.partial(_gqa_kernel, online=online),
        grid=(bsz, sq // tq),
        in_specs=[pl.BlockSpec((1, A_HEADS, HEAD_DIM, tq), lambda b, i: (b, 0, 0, i)),
                  pl.BlockSpec((1, A_KV_HEADS, sk, HEAD_DIM), lambda b, i: (b, 0, 0, 0)),
                  pl.BlockSpec((1, n_chunks, vr, tk), lambda b, i: (b, 0, 0, 0))],
        out_specs=pl.BlockSpec((1, A_Q, tq), lambda b, i: (b, 0, i)),
        out_shape=jax.ShapeDtypeStruct((bsz, A_Q, sq), CDT),
        scratch_shapes=[pltpu.VMEM((group, 1, tq), F32), pltpu.VMEM((group, 1, tq), F32),
                        pltpu.VMEM((group, HEAD_DIM, tq), F32), pltpu.VMEM((tk, tq), F32)],
        compiler_params=_cparams("arbitrary", "arbitrary"),
        name="gqa_online" if online else "gqa",
    )(qT, k, vT)


def _key_chunks(vT, tk):
    bsz, vr, sk = vT.shape
    return vT.reshape(bsz, vr, sk // tk, tk).transpose(0, 2, 1, 3)


def _gqa(qT, k, vT, gain_q, gain_k, *, tq):
    bound = (HEAD_DIM ** 0.5 * LOG2E) * jnp.max(jnp.abs(gain_q)) * jnp.max(jnp.abs(gain_k))
    return lax.cond(bound < MAX_UNSHIFTED_LOG2_SCORE,
                    functools.partial(_gqa_call, tq=tq, online=False),
                    functools.partial(_gqa_call, tq=tq, online=True),
                    qT, k, vT)


def _pair_queries(qp):
    lane = lax.broadcasted_iota(jnp.int32, qp.shape, 1)
    zero = jnp.zeros_like(qp)
    return jnp.concatenate([jnp.where(lane < HEAD_DIM, qp, zero), jnp.where(lane >= HEAD_DIM, qp, zero)], axis=0)


def _pair_finish(pv, l):
    n = pv.shape[0] // 2
    pv = pv / l
    lane = lax.broadcasted_iota(jnp.int32, (n, 2 * HEAD_DIM), 1)
    return jnp.where(lane < HEAD_DIM, pv[:n], pv[n:])


def _natten_kernel(q_ref, k_ref, v_ref, kc_ref, vc_ref, bias_ref, o_ref, *, rows_per_step, n_rows):
    i = pl.program_id(1)
    win = WIN_R * GRID_W
    n_pairs = B_HEADS // 2

    def window(rr):
        r = i * rows_per_step + rr
        r0 = jnp.clip(r - WIN_R // 2, 0, n_rows - WIN_R)
        return pl.multiple_of(r0 * GRID_W, GRID_W), r0 - r + (WIN_R - 1)

    def scores(rr, p):
        start, cfg = window(rr)
        cols = slice(2 * HEAD_DIM * p, 2 * HEAD_DIM * (p + 1))
        qs = _pair_queries(q_ref[0, 0, GRID_W * rr:GRID_W * (rr + 1), cols])
        s_loc = _dot_nt(qs, k_ref[0, 0, pl.ds(start, win), cols]) + bias_ref[cfg, p]
        return s_loc, _dot_nt(qs, kc_ref[0, 0, :, cols])

    def finish(rr, p, s_loc, s_ctx):
        start, _ = window(rr)
        cols = slice(2 * HEAD_DIM * p, 2 * HEAD_DIM * (p + 1))
        m = jnp.maximum(jnp.max(s_loc, axis=-1, keepdims=True), jnp.max(s_ctx, axis=-1, keepdims=True))
        e_loc = jnp.exp2(s_loc - m)
        e_ctx = jnp.exp2(s_ctx - m)
        l = jnp.sum(e_loc, axis=-1, keepdims=True) + jnp.sum(e_ctx, axis=-1, keepdims=True)
        pv = _dot(e_loc, v_ref[0, 0, pl.ds(start, win), cols]) + _dot(e_ctx, vc_ref[0, 0, :, cols])
        return _pair_finish(pv, l)

    units = [(rr, p) for rr in range(rows_per_step) for p in range(n_pairs)]
    ahead = 2
    pending = [scores(*u) for u in units[:ahead]]
    outs = []
    for idx, (rr, p) in enumerate(units):
        cur = pending.pop(0)
        if idx + ahead < len(units):
            pending.append(scores(*units[idx + ahead]))
        outs.append(finish(rr, p, *cur))
        if p == n_pairs - 1:
            o_ref[0, GRID_W * rr:GRID_W * (rr + 1), :] = jnp.concatenate(outs, axis=-1).astype(o_ref.dtype)
            outs = []


def _natten(proj, cproj, bias, *, rows_per_step):
    bsz, _, s, _ = proj.shape
    lc = cproj.shape[2]
    n_rows = s // GRID_W
    tq = rows_per_step * GRID_W
    whole = lambda blk: (lambda b, i: (b, blk, 0, 0))
    return pl.pallas_call(
        functools.partial(_natten_kernel, rows_per_step=rows_per_step, n_rows=n_rows),
        grid=(bsz, s // tq),
        in_specs=[pl.BlockSpec((1, 1, tq, CB), lambda b, i: (b, P_QB, i, 0)),
                  pl.BlockSpec((1, 1, s, CB), whole(P_KB), pipeline_mode=pl.Buffered(1)),
                  pl.BlockSpec((1, 1, s, CB), whole(P_VB), pipeline_mode=pl.Buffered(1)),
                  pl.BlockSpec((1, 1, lc, CB), whole(P_KB)),
                  pl.BlockSpec((1, 1, lc, CB), whole(P_VB)),
                  pl.BlockSpec(bias.shape, lambda b, i: (0, 0, 0, 0), pipeline_mode=pl.Buffered(1))],
        out_specs=pl.BlockSpec((1, tq, B_W), lambda b, i: (b, i, 0)),
        out_shape=jax.ShapeDtypeStruct((bsz, s, B_W), CDT),
        compiler_params=_cparams("arbitrary", "arbitrary"),
        name="natten",
    )(proj, proj, proj, cproj, cproj, bias)


def _ctx_attn_b_kernel(q_ref, k_ref, v_ref, o_ref):
    outs = []
    for p in range(B_HEADS // 2):
        cols = slice(2 * HEAD_DIM * p, 2 * HEAD_DIM * (p + 1))
        qs = _pair_queries(q_ref[0, 0, :, cols])
        s = _dot_nt(qs, k_ref[0, 0, :, cols])
        e = jnp.exp2(s - jnp.max(s, axis=-1, keepdims=True))
        outs.append(_pair_finish(_dot(e, v_ref[0, 0, :, cols]), jnp.sum(e, axis=-1, keepdims=True)))
    o_ref[0] = jnp.concatenate(outs, axis=-1).astype(o_ref.dtype)


def _ctx_attn_b(cproj):
    bsz, _, lc, _ = cproj.shape
    blk = lambda c: pl.BlockSpec((1, 1, lc, CB), lambda b: (b, c, 0, 0))
    return pl.pallas_call(
        _ctx_attn_b_kernel,
        grid=(bsz,),
        in_specs=[blk(P_QB), blk(P_KB), blk(P_VB)],
        out_specs=pl.BlockSpec((1, lc, B_W), lambda b: (b, 0, 0)),
        out_shape=jax.ShapeDtypeStruct((bsz, lc, B_W), CDT),
        compiler_params=_cparams("arbitrary"),
        name="ctx_attn_b",
    )(cproj, cproj, cproj)


def _natten_bias(rpb):
    h = rpb.shape[0]
    nd = 2 * WIN_C - 1
    lead = GRID_W - WIN_C
    w = jnp.pad(rpb.astype(F32), ((0, 0), (0, 0), (lead, 2 * GRID_W - lead - nd)))
    t = jnp.tile(w, (1, 1, GRID_W))[..., :GRID_W * (2 * GRID_W - 1)]
    toe = t.reshape(h, 2 * WIN_R - 1, GRID_W, 2 * GRID_W - 1)[..., GRID_W - 1:]
    qc = jnp.arange(GRID_W)
    c0 = jnp.clip(qc - WIN_C // 2, 0, GRID_W - WIN_C)
    inwin = (qc[None, :] >= c0[:, None]) & (qc[None, :] < c0[:, None] + WIN_C)
    toe = jnp.where(inwin, toe * LOG2E, -1e30)
    toe = toe.transpose(0, 2, 1, 3)
    tab = jnp.stack([toe[:, :, cfg:cfg + WIN_R] for cfg in range(WIN_R)])
    return tab.reshape(WIN_R, h // 2, 2 * GRID_W, WIN_R * GRID_W)


def _gdn_feat_kernel(x_ref, p_ref, n_ref, w_ref, o_ref, xs_ref):
    i = pl.program_id(1)
    c = pl.program_id(2)
    tm = x_ref.shape[2]
    hp = BF16_SUBLANES
    e = F32_SUBLANES
    xs_ref[0:e] = jnp.where(i > 0, p_ref[0, 0].astype(F32), 0.0)[hp - e:]
    xs_ref[e:e + tm] = x_ref[0, 0].astype(F32)
    xs_ref[e + tm:] = jnp.where(i < pl.num_programs(1) - 1, n_ref[0, 0].astype(F32), 0.0)[:e]
    w = w_ref[...]
    half = CONV_K // 2
    y = w[0:1] * xs_ref[e - half:e - half + tm]
    for j in range(1, CONV_K):
        y = y + w[j:j + 1] * xs_ref[e - half + j:e - half + j + tm]
    y = _silu(y)

    @pl.when(c == 2)
    def _():
        o_ref[0] = y.astype(o_ref.dtype)

    @pl.when(c < 2)
    def _():
        scale = jnp.where(c == 0, C_DK ** -0.5, 1.0)
        parts = []
        for h in range(C_HEADS):
            yh = y[:, C_DK * h:C_DK * (h + 1)]
            parts.append(yh * (lax.rsqrt(jnp.sum(yh * yh, axis=-1, keepdims=True) + EPS) * scale))
        o_ref[0] = jnp.concatenate(parts, axis=-1).astype(o_ref.dtype)


def _gdn_features(proj, conv_w, *, tm):
    bsz, _, s, _ = proj.shape
    hp = BF16_SUBLANES
    nh = s // hp
    return pl.pallas_call(
        _gdn_feat_kernel,
        grid=(bsz, s // tm, 3),
        in_specs=[pl.BlockSpec((1, 1, tm, CB), lambda b, i, c: (b, P_GDN0 + c, i, 0)),
                  pl.BlockSpec((1, 1, hp, CB),
                               lambda b, i, c: (b, P_GDN0 + c, jnp.maximum(i * (tm // hp) - 1, 0), 0)),
                  pl.BlockSpec((1, 1, hp, CB),
                               lambda b, i, c: (b, P_GDN0 + c, jnp.minimum((i + 1) * (tm // hp), nh - 1), 0)),
                  pl.BlockSpec((CONV_K, CB), lambda b, i, c: (0, c))],
        out_specs=pl.BlockSpec((1, tm, CB), lambda b, i, c: (b, i, c)),
        out_shape=jax.ShapeDtypeStruct((bsz, s, 3 * CB), CDT),
        scratch_shapes=[pltpu.VMEM((tm + 2 * F32_SUBLANES, CB), F32)],
        compiler_params=_cparams("arbitrary", "arbitrary", "arbitrary"),
        name="gdn_features",
    )(proj, proj, proj, conv_w)


def _block_diag(x, n, blk):
    lane_blk = lax.broadcasted_iota(jnp.int32, (1, n * blk), 1) // blk
    return jnp.concatenate([jnp.where(lane_blk == h, x, 0.0) for h in range(n)], axis=0)


def _gdn_prepare(d, feat, bgc, bgr):
    nh = C_HEADS
    col0 = d * nh
    last = CHUNK - 1 if d == 0 else 0
    q = feat[:, :C_K].astype(F32)
    k = feat[:, C_K:2 * C_K].astype(F32)
    v = feat[:, 2 * C_K:].astype(F32)

    def wide(c0, width):
        return jnp.concatenate([jnp.broadcast_to(bgc[:, c0 + h:c0 + h + 1], (CHUNK, width)) for h in range(nh)],
                               axis=1)

    beta = wide(col0, C_DK)
    gcol = wide(N_GATE_DIRS + col0, C_DK)
    gcol_c = wide(N_GATE_DIRS + col0, CHUNK)
    grow_c = jnp.concatenate([bgr[N_GATE_DIRS + col0 + h:N_GATE_DIRS + col0 + h + 1, :] for h in range(nh)], axis=1)
    glast = gcol[last:last + 1, :]
    egc = jnp.exp(gcol)
    kb = k * beta
    ii = lax.broadcasted_iota(jnp.int32, (CHUNK, nh * CHUNK), 0)
    jj = lax.broadcasted_iota(jnp.int32, (CHUNK, nh * CHUNK), 1) % CHUNK
    incl = (ii >= jj) if d == 0 else (ii <= jj)
    strict = (ii > jj) if d == 0 else (ii < jj)
    r = _dot_nt(jnp.concatenate([kb, q], axis=0), _block_diag(k, nh, C_DK))
    decay = jnp.exp(jnp.where(incl, gcol_c - grow_c, -jnp.inf))
    nil = jnp.where(strict, -r[:CHUNK] * decay, 0.0)
    rhs_w = jnp.concatenate(
        [jnp.concatenate([(v * beta)[:, C_DV * h:C_DV * (h + 1)], (kb * egc)[:, C_DK * h:C_DK * (h + 1)]], axis=1)
         for h in range(nh)], axis=0).astype(CDT)
    k_dec = k * jnp.exp(glast - gcol)
    return dict(
        tinv=jnp.where(ii == jj, 1.0, 0.0) + nil,
        power=nil,
        aqk=r[CHUNK:] * decay,
        rhs_w=rhs_w,
        qd=(q * egc).astype(CDT),
        k_decT=[k_dec[:, C_DK * h:C_DK * (h + 1)].T.astype(CDT) for h in range(nh)],
        eglast=jnp.exp(glast),
    )


def _gdn_scan_kernel(ff_ref, fr_ref, bgf_ref, bgr_ref, bgTf_ref, bgTr_ref, s0_ref,
                     of_ref, or_ref, sN_ref, s_ref, *, n_chunks):
    step = pl.program_id(0)
    bsz = ff_ref.shape[0]
    nh = C_HEADS
    io = ((ff_ref, bgf_ref, bgTf_ref, of_ref), (fr_ref, bgr_ref, bgTr_ref, or_ref))

    @pl.when(step == 0)
    def _():
        s_ref[...] = s0_ref[...]

    units = {}
    for c in range(n_chunks):
        rows = slice(CHUNK * c, CHUNK * (c + 1))
        for b in range(bsz):
            for d in range(2):
                f_ref, bg_ref, bgT_ref, _ = io[d]
                units[b, d, c] = _gdn_prepare(d, f_ref[b, rows, :], bg_ref[b, rows, :], bgT_ref[b, :, rows])
    us = list(units.values())

    bd = lambda x: _block_diag(x, nh, CHUNK)
    n_sq = int(math.log2(CHUNK)) - 1
    for u in us:
        u["power"] = _dot(u["power"], bd(u["power"]))
    for _ in range(n_sq - 1):
        for u in us:
            r = _dot(jnp.concatenate([u["tinv"], u["power"]], axis=0), bd(u["power"]))
            u["tinv"] = u["tinv"] + r[:CHUNK]
            u["power"] = r[CHUNK:]
    for u in us:
        u["tinv"] = u["tinv"] + _dot(u["tinv"], bd(u["power"]))
    for u in us:
        w = _dot(bd(u["tinv"]), u["rhs_w"])
        u["w_v"] = [w[CHUNK * h:CHUNK * (h + 1), :C_DV] for h in range(nh)]
        u["k_cd"] = [w[CHUNK * h:CHUNK * (h + 1), C_DV:].astype(CDT) for h in range(nh)]

    state = {(b, d, h): s_ref[b, d, h] for b in range(bsz) for d in range(2) for h in range(nh)}
    for t in range(n_chunks):
        act = [(b, d, t if d == 0 else n_chunks - 1 - t) for b in range(bsz) for d in range(2)]
        v_new, q_state = {}, {}
        for key in act:
            u = units[key]
            for h in range(nh):
                x = _dot(jnp.concatenate([u["k_cd"][h], u["qd"][:, C_DK * h:C_DK * (h + 1)]], axis=0),
                         state[key[0], key[1], h])
                v_new[key, h] = u["w_v"][h] - x[:CHUNK]
                q_state[key, h] = x[CHUNK:]
        for key in act:
            u = units[key]
            for h in range(nh):
                sk = (key[0], key[1], h)
                state[sk] = (state[sk] * u["eglast"][:, C_DV * h:C_DV * (h + 1)]
                             + _dot(u["k_decT"][h], v_new[key, h]))
        for key in act:
            b, d, c = key
            av = _dot(bd(units[key]["aqk"]), jnp.concatenate([v_new[key, h] for h in range(nh)], axis=0))
            for h in range(nh):
                o_ref = io[d][3]
                o_ref[b, CHUNK * c:CHUNK * (c + 1), C_DV * h:C_DV * (h + 1)] = (
                    q_state[key, h] + av[CHUNK * h:CHUNK * (h + 1)]).astype(o_ref.dtype)
    for (b, d, h), val in state.items():
        s_ref[b, d, h] = val

    @pl.when(step == pl.num_programs(0) - 1)
    def _():
        sN_ref[...] = s_ref[...]


def _gdn_scan(feat, bg, bgT, state0, *, n_chunks):
    bsz, s, _ = feat.shape
    tb = n_chunks * CHUNK
    ns = s // tb
    fwd = lambda i: (0, i, 0)
    rev = lambda i: (0, ns - 1 - i, 0)
    fwdT = lambda i: (0, 0, i)
    revT = lambda i: (0, 0, ns - 1 - i)
    st = lambda i: (0, 0, 0, 0, 0)
    st_shape = (bsz, 2, C_HEADS, C_DK, C_DV)
    return pl.pallas_call(
        functools.partial(_gdn_scan_kernel, n_chunks=n_chunks),
        grid=(ns,),
        in_specs=[pl.BlockSpec((bsz, tb, 3 * CB), fwd), pl.BlockSpec((bsz, tb, 3 * CB), rev),
                  pl.BlockSpec((bsz, tb, LANES), fwd), pl.BlockSpec((bsz, tb, LANES), rev),
                  pl.BlockSpec((bsz, 2 * N_GATE_DIRS, tb), fwdT), pl.BlockSpec((bsz, 2 * N_GATE_DIRS, tb), revT),
                  pl.BlockSpec(st_shape, st)],
        out_specs=[pl.BlockSpec((bsz, tb, C_V), fwd), pl.BlockSpec((bsz, tb, C_V), rev),
                   pl.BlockSpec(st_shape, st)],
        out_shape=[jax.ShapeDtypeStruct((bsz, s, C_V), F32), jax.ShapeDtypeStruct((bsz, s, C_V), F32),
                   jax.ShapeDtypeStruct(st_shape, F32)],
        scratch_shapes=[pltpu.VMEM(st_shape, F32)],
        compiler_params=_cparams("arbitrary"),
        name="gdn_scan",
    )(feat, feat, bg, bg, bgT, bgT, state0)


def _merge_kernel(x_ref, oaT_ref, ob_ref, of_ref, or_ref, z_ref, g0_ref, g1_ref, g2_ref, onorm_ref,
                  wb_ref, wo_ref, gt_ref, out_ref, *, row_mul, row_off):
    b = pl.program_id(0)
    oc = of_ref[0].astype(F32) + or_ref[0].astype(F32)
    parts = []
    for h in range(C_HEADS):
        oh = oc[:, C_DV * h:C_DV * (h + 1)]
        parts.append(oh * lax.rsqrt(jnp.mean(oh * oh, axis=-1, keepdims=True) + EPS) * onorm_ref[...])
    yc = jnp.concatenate(parts, axis=-1) * _silu(z_ref[0, 0].astype(F32))
    gate = lambda g_ref: _sigmoid(jnp.concatenate([g_ref[0, 0], g_ref[0, 1]], axis=-1).astype(F32))
    m = (gate(g0_ref) * _dot_tn(oaT_ref[0], wb_ref[0, 0])
         + gate(g1_ref) * _dot(ob_ref[0], wb_ref[0, 1])
         + gate(g2_ref) * _dot(yc, wb_ref[0, 2]))
    gt = gt_ref[pl.ds(b * row_mul + row_off, 1), :]
    out_ref[0] = x_ref[0] + gt * _dot(m, wo_ref[0])


def _merge(x, oaT, ob, of, orr, proj, onorm, wb, wo, layer, mod_l, *, row_mul, row_off, tm):
    bsz, s, d = x.shape
    assert d == 2 * CB
    tok = lambda blk: (lambda b, i: (b, i, blk))
    gate = lambda n: pl.BlockSpec((1, 2, tm, CB), lambda b, i: (b, n, i, 0))
    return pl.pallas_call(
        functools.partial(_merge_kernel, row_mul=row_mul, row_off=row_off),
        grid=(bsz, s // tm),
        in_specs=[pl.BlockSpec((1, tm, d), tok(0)),
                  pl.BlockSpec((1, A_Q, tm), lambda b, i: (b, 0, i)),
                  pl.BlockSpec((1, tm, B_W), tok(0)),
                  pl.BlockSpec((1, tm, C_V), tok(0)),
                  pl.BlockSpec((1, tm, C_V), tok(0)),
                  pl.BlockSpec((1, 1, tm, CB), lambda b, i: (b, P_Z, i, 0)),
                  gate(P_GATE_PAIRS[0]),
                  gate(P_GATE_PAIRS[1]),
                  gate(P_GATE_PAIRS[2]),
                  pl.BlockSpec((1, C_DV), lambda b, i: (0, 0)),
                  pl.BlockSpec((1,) + wb.shape[1:], lambda b, i: (layer, 0, 0, 0)),
                  pl.BlockSpec((1,) + wo.shape[1:], lambda b, i: (layer, 0, 0)),
                  pl.BlockSpec((8, d), lambda b, i: (0, 2))],
        out_specs=pl.BlockSpec((1, tm, d), tok(0)),
        out_shape=jax.ShapeDtypeStruct((bsz, s, d), F32),
        compiler_params=_cparams("arbitrary", "arbitrary"),
        name="merge",
    )(x, oaT, ob, of, orr, proj, proj, proj, proj, onorm.reshape(1, C_DV), wb, wo, mod_l)


def _mlp_kernel(x_ref, sh_ref, sc_ref, gt_ref, gain_ref, w1_ref, w2_ref, fin_ref, out_ref, h_ref, acc_ref,
                *, row_mul, row_off, final_norm):
    b = pl.program_id(0)
    f = pl.program_id(2)
    row = b * row_mul + row_off

    @pl.when(f == 0)
    def _():
        x = x_ref[0]
        r = lax.rsqrt(jnp.mean(x * x, axis=-1, keepdims=True) + EPS)
        h = (x * r * gain_ref[...]) * (1.0 + sc_ref[pl.ds(row, 1), :]) + sh_ref[pl.ds(row, 1), :]
        h_ref[...] = h.astype(CDT)
        acc_ref[...] = jnp.zeros_like(acc_ref)

    a = jnp.maximum(jnp.dot(h_ref[...], w1_ref[0], preferred_element_type=F32), 0.0)
    acc_ref[...] += _dot(a * a, w2_ref[0])

    @pl.when(f == pl.num_programs(2) - 1)
    def _():
        y = x_ref[0] + gt_ref[pl.ds(row, 1), :] * acc_ref[...]
        if final_norm:
            y = y * lax.rsqrt(jnp.mean(y * y, axis=-1, keepdims=True) + EPS) * fin_ref[...]
        out_ref[0] = y


def _mlp(x, mod_l, gain, w1, w2, layer, fin, *, row_mul, row_off, tm, tf, final_norm):
    bsz, s, d = x.shape
    dff = w1.shape[2]
    return pl.pallas_call(
        functools.partial(_mlp_kernel, row_mul=row_mul, row_off=row_off, final_norm=final_norm),
        grid=(bsz, s // tm, dff // tf),
        in_specs=[pl.BlockSpec((1, tm, d), lambda b, i, f: (b, i, 0)),
                  pl.BlockSpec((8, d), lambda b, i, f: (0, 3)),
                  pl.BlockSpec((8, d), lambda b, i, f: (0, 4)),
                  pl.BlockSpec((8, d), lambda b, i, f: (0, 5)),
                  pl.BlockSpec((1, d), lambda b, i, f: (0, 0)),
                  pl.BlockSpec((1, d, tf), lambda b, i, f: (layer, 0, f)),
                  pl.BlockSpec((1, tf, d), lambda b, i, f: (layer, f, 0)),
                  pl.BlockSpec((1, d), lambda b, i, f: (0, 0))],
        out_specs=pl.BlockSpec((1, tm, d), lambda b, i, f: (b, i, 0)),
        out_shape=jax.ShapeDtypeStruct((bsz, s, d), F32),
        scratch_shapes=[pltpu.VMEM((tm, d), CDT), pltpu.VMEM((tm, d), F32)],
        compiler_params=_cparams("arbitrary", "arbitrary", "arbitrary"),
        name="mlp",
    )(x, mod_l, mod_l, mod_l, gain.reshape(1, d), w1, w2, fin.reshape(1, d))


def _split_w_in(w_in):
    w = w_in.astype(CDT)
    small = 2 * N_GATE_DIRS
    pad = jnp.zeros(w.shape[:-1] + (W_TAIL_GATE - small,), CDT)
    return w, jnp.concatenate([w[..., W_MAIN:W_MAIN + small], pad, w[..., W_MAIN + small:]], axis=-1)


def _rope_tables(n_tok, gain_q, gain_k, rotate):
    if rotate:
        t = jnp.arange(n_tok)
        row = (t // GRID_W).astype(F32)
        col = (t % GRID_W).astype(F32)
        axis_dim = HEAD_DIM // 2
        freqs = ROPE_BASE ** (-jnp.arange(0, axis_dim, 2, dtype=F32) / axis_dim)
        ang = jnp.concatenate([row[:, None] * freqs, col[:, None] * freqs], axis=-1)
        cos = jnp.repeat(jnp.cos(ang), 2, axis=-1).T
        sin = jnp.repeat(jnp.sin(ang), 2, axis=-1).T
    else:
        cos = jnp.ones((HEAD_DIM, n_tok), F32)
        sin = jnp.zeros((HEAD_DIM, n_tok), F32)
    even = (jnp.arange(HEAD_DIM) % 2 == 0)[:, None]

    def tables(g, scale):
        g = g.astype(F32) * scale
        return [cos * g[:, None],
                jnp.where(even, -sin * jnp.roll(g, -1)[:, None], 0.0),
                jnp.where(even, 0.0, sin * jnp.roll(g, 1)[:, None])]

    return jnp.stack(tables(gain_q, HEAD_DIM ** -0.5 * LOG2E) + tables(gain_k, 1.0))


def _decay_params(a_log, dt_bias):
    row = lambda v: jnp.zeros((LANES,), F32).at[N_GATE_DIRS:2 * N_GATE_DIRS].set(v.reshape(-1).astype(F32))
    return jnp.zeros((8, LANES), F32).at[0].set(row(a_log)).at[1].set(row(dt_bias))


def kernel(x, c, ctx, c_ctx, w_mod, b_mod, norm_mix, w_in, q_norm_a, k_norm_a, rpb_b, conv_c, a_log_c, dt_bias_c,
           o_norm_c, w_branch, w_out, norm_ffn, w_ffn1, w_ffn2, norm_final):
    bsz, s, d = x.shape
    lc = ctx.shape[1]
    depth = w_mod.shape[0]
    assert bsz < 8 and s % 1024 == 0 and lc == 256 and d % CB == 0

    tm = 1024
    tk = next(t for t in (768, 640, 512, 384, 256, 128) if (s + lc) % t == 0)
    cc = jnp.zeros((8, d), F32).at[:bsz].set(c).at[bsz].set(c_ctx)
    mod = _modulation(cc, w_mod, b_mod)
    w_all, w_tail = _split_w_in(w_in)
    wb = w_branch.astype(CDT)
    wo = w_out.astype(CDT)
    w1 = w_ffn1.astype(CDT)
    w2 = w_ffn2.astype(CDT)
    lat = dict(row_mul=1, row_off=0)
    con = dict(row_mul=0, row_off=bsz)
    zero_state = jnp.zeros((bsz, 2, C_HEADS, C_DK, C_DV), F32)

    xc = ctx
    for l in range(depth):
        need_ctx = l < depth - 1
        par = _decay_params(a_log_c[l], dt_bias_c[l])
        rope_l = _rope_tables(s, q_norm_a[l], k_norm_a[l], True)
        rope_c = _rope_tables(lc, q_norm_a[l], k_norm_a[l], False)
        bias = _natten_bias(rpb_b[l])
        conv_w = conv_c[l].astype(F32)

        P = _inproj(x, mod[l], norm_mix[l], w_all, w_tail, l, rope_l, par, tm=512, **lat)
        Pc = _inproj(xc, mod[l], norm_mix[l], w_all, w_tail, l, rope_c, par, tm=lc, **con)

        k_all = jnp.concatenate([P["k"], Pc["k"]], axis=2)
        vT_all = _key_chunks(jnp.concatenate([P["vT"], Pc["vT"]], axis=2), tk)
        o_aT = _gqa(P["qT"], k_all, vT_all, q_norm_a[l], k_norm_a[l], tq=512)
        o_b = _natten(P["proj"], Pc["proj"], bias, rows_per_step=8)
        featc = _gdn_features(Pc["proj"], conv_w, tm=lc)
        feat = _gdn_features(P["proj"], conv_w, tm=512)
        oc_f, oc_r, state = _gdn_scan(featc, Pc["bg"], Pc["bgT"], zero_state, n_chunks=2)
        o_f, o_r, _ = _gdn_scan(feat, P["bg"], P["bgT"], state, n_chunks=4)

        last = l == depth - 1
        x = _merge(x, o_aT, o_b, o_f, o_r, P["proj"], o_norm_c[l], wb, wo, l, mod[l], tm=tm, **lat)
        x = _mlp(x, mod[l], norm_ffn[l], w1, w2, l, norm_final, tm=tm, tf=1024, final_norm=last, **lat)
        if need_ctx:
            oc_aT = _gqa_call(Pc["qT"], Pc["k"], _key_chunks(Pc["vT"], lc), tq=lc, online=True)
            oc_b = _ctx_attn_b(Pc["proj"])
            xc = _merge(xc, oc_aT, oc_b, oc_f, oc_r, Pc["proj"], o_norm_c[l], wb, wo, l, mod[l], tm=lc, **con)
            xc = _mlp(xc, mod[l], norm_ffn[l], w1, w2, l, norm_final, tm=lc, tf=1024, final_norm=False, **con)
    return x
```
